```python
import math
import jax, jax.numpy as jnp
from jax import lax
import numpy as np

D_MODEL = 2048
BATCH = 16
SEQ = 2048
DEPTH = 2

CHUNK = 64
NORM_EPS = 1e-6
MLSTM_HEADS = 4
MLSTM_DQK = 128
MLSTM_DV = 256
GATE_SOFTCAP = 15.0
RET_HEADS = 4
RET_DK = 256
RET_DV = 256
ROPE_BASE = 10000.0
GDN_QK_HEADS = 16
GDN_V_HEADS = 32
GDN_DK = 128
GDN_DV = 128
CONV_WIDTH = 4
D_FF = -(-8 * D_MODEL // (3 * 256)) * 256

MLSTM_QK_W = MLSTM_HEADS * MLSTM_DQK
MLSTM_V_W = MLSTM_HEADS * MLSTM_DV
RET_QK_W = RET_HEADS * RET_DK
RET_V_W = RET_HEADS * RET_DV
L0_SPLITS = [MLSTM_QK_W, MLSTM_QK_W, MLSTM_V_W, MLSTM_V_W, 2 * MLSTM_HEADS,
             RET_QK_W, RET_QK_W, RET_V_W, RET_V_W]
L0_IN_W = sum(L0_SPLITS)
L0_MIX_W = MLSTM_V_W + RET_V_W
GDN_QK_W = GDN_QK_HEADS * GDN_DK
GDN_V_W = GDN_V_HEADS * GDN_DV
GDN_CONV_C = 2 * GDN_QK_W + GDN_V_W
L1_SPLITS = [GDN_CONV_C, GDN_V_W, GDN_V_HEADS, GDN_V_HEADS]
L1_IN_W = sum(L1_SPLITS)

kernel_name = 'hybrid_mlstm_retention_gdn_trunk'


def _rmsnorm(x, w):
    xf = x.astype(jnp.float32)
    y = xf * lax.rsqrt(jnp.mean(xf * xf, axis=-1, keepdims=True) + NORM_EPS)
    return (y * w.astype(jnp.float32)).astype(x.dtype)


def _head_rmsnorm(t):
    return t * lax.rsqrt(jnp.mean(t * t, axis=-1, keepdims=True) + NORM_EPS)


def _l2norm(t):
    return t * lax.rsqrt(jnp.sum(t * t, axis=-1, keepdims=True) + NORM_EPS)


def _split(h, sizes):
    return jnp.split(h, [int(c) for c in np.cumsum(sizes)[:-1]], axis=-1)


def _to_chunks(t):
    b, s, h, d = t.shape
    return t.reshape(b, s // CHUNK, CHUNK, h, d).transpose(0, 3, 1, 2, 4)


def _from_chunks(t):
    b, h, n, l, d = t.shape
    return t.transpose(0, 2, 3, 1, 4).reshape(b, n * l, h, d)


def _gate_chunks(t):
    b, s, h = t.shape
    return t.reshape(b, s // CHUNK, CHUNK, h).transpose(0, 3, 1, 2)


def _causal_mask(strict=False):
    return jnp.tril(jnp.ones((CHUNK, CHUNK), dtype=bool), -1 if strict else 0)


def _chunk_major(t):
    return jnp.moveaxis(t, 2, 0)


def _rotary(t, positions):
    d = t.shape[-1]
    inv_freq = ROPE_BASE ** (-jnp.arange(0, d, 2, dtype=jnp.float32) / d)
    ang = positions.astype(jnp.float32)[..., None] * inv_freq
    cos = jnp.cos(ang)[:, :, None, :]
    sin = jnp.sin(ang)[:, :, None, :]
    t1, t2 = t[..., 0::2], t[..., 1::2]
    return jnp.stack([t1 * cos - t2 * sin, t1 * sin + t2 * cos], axis=-1).reshape(t.shape)


def _mlstm_chunkwise(q, k, v, log_i, log_f):
    causal = _causal_mask()
    b_cum = jnp.cumsum(log_f, axis=-1)
    g = b_cum[..., -1]
    log_d = jnp.where(causal, b_cum[..., :, None] - b_cum[..., None, :] + log_i[..., None, :], -jnp.inf)
    m_intra = jnp.max(log_d, axis=-1)
    log_w = g[..., None] - b_cum + log_i
    m_w = jnp.max(log_w, axis=-1)
    bsz, nh, _, _, dk = q.shape
    dv = v.shape[-1]

    def step(carry, xs):
        c_mat, n_vec, m_run = carry
        q_c, k_c, v_c, lw_c, mw_c, g_c = xs
        qc_out = jnp.einsum('bhld,bhde->bhle', q_c, c_mat)
        qn_out = jnp.einsum('bhld,bhd->bhl', q_c, n_vec)
        m_new = jnp.maximum(g_c + m_run, mw_c)
        decay = jnp.exp(g_c + m_run - m_new)
        kw = k_c * jnp.exp(lw_c - m_new[..., None])[..., None]
        c_mat = decay[..., None, None] * c_mat + jnp.einsum('bhld,bhle->bhde', kw, v_c)
        n_vec = decay[..., None] * n_vec + jnp.sum(kw, axis=-2)
        return (c_mat, n_vec, m_new), (qc_out, qn_out, m_run)

    init = (jnp.zeros((bsz, nh, dk, dv), jnp.float32),
            jnp.zeros((bsz, nh, dk), jnp.float32),
            jnp.full((bsz, nh), -jnp.inf, dtype=jnp.float32))
    _, (qc, qn, m_prev) = lax.scan(
        step, init,
        (_chunk_major(q), _chunk_major(k), _chunk_major(v), _chunk_major(log_w), _chunk_major(m_w), _chunk_major(g)))
    qc = jnp.moveaxis(qc, 0, 2)
    qn = jnp.moveaxis(qn, 0, 2)
    m_prev = jnp.moveaxis(m_prev, 0, 2)
    m_inter = b_cum + m_prev[..., None]
    m_t = jnp.maximum(m_inter, m_intra)
    inter_scale = jnp.exp(m_inter - m_t)
    s = jnp.einsum('bhnld,bhnsd->bhnls', q, k) * jnp.exp(log_d - m_t[..., None])
    num = jnp.einsum('bhnls,bhnse->bhnle', s, v) + inter_scale[..., None] * qc
    den = jnp.sum(s, axis=-1) + inter_scale * qn
    return num / jnp.maximum(jnp.abs(den), jnp.exp(-m_t))[..., None]


def _retention_chunkwise(q, k, v, log_gamma):
    causal = _causal_mask()
    pos = jnp.arange(CHUNK, dtype=jnp.float32)
    rel = jnp.where(causal, pos[:, None] - pos[None, :], 0.0)
    d_intra = jnp.where(causal, jnp.exp(rel * log_gamma[:, None, None]), 0.0)
    scores = jnp.einsum('bhnld,bhnsd->bhnls', q, k) * d_intra[:, None]
    intra = jnp.einsum('bhnls,bhnse->bhnle', scores, v)
    lg = log_gamma[:, None]
    k_w = k * jnp.exp((CHUNK - 1 - pos) * lg)[:, None, :, None]
    q_w = q * jnp.exp((pos + 1) * lg)[:, None, :, None]
    chunk_decay = jnp.exp(CHUNK * log_gamma)[None, :, None, None]

    def step(r, xs):
        qw_c, kw_c, v_c = xs
        out = jnp.einsum('bhld,bhde->bhle', qw_c, r)
        r = chunk_decay * r + jnp.einsum('bhld,bhle->bhde', kw_c, v_c)
        return r, out

    bsz, nh, _, _, dk = q.shape
    init = jnp.zeros((bsz, nh, dk, v.shape[-1]), jnp.float32)
    _, inter = lax.scan(step, init, (_chunk_major(q_w), _chunk_major(k_w), _chunk_major(v)))
    return intra + jnp.moveaxis(inter, 0, 2)


def _gated_delta_chunkwise(q, k, v, log_alpha, beta):
    causal = _causal_mask()
    strict = _causal_mask(strict=True)
    g_cum = jnp.cumsum(log_alpha, axis=-1)
    decay = jnp.exp(jnp.where(causal, g_cum[..., :, None] - g_cum[..., None, :], -jnp.inf))
    k_beta = k * beta[..., None]
    a = jnp.where(strict, jnp.einsum('bhnld,bhnsd->bhnls', k_beta, k) * decay, 0.0)
    dv = v.shape[-1]
    rhs = jnp.concatenate([v * beta[..., None], k_beta * jnp.exp(g_cum)[..., None]], axis=-1)
    sol = lax.linalg.triangular_solve(a, rhs, left_side=True, lower=True, unit_diagonal=True)
    u, w = sol[..., :dv], sol[..., dv:]
    attn = jnp.where(causal, jnp.einsum('bhnld,bhnsd->bhnls', q, k) * decay, 0.0)
    q_g = q * jnp.exp(g_cum)[..., None]
    k_tail = k * jnp.exp(g_cum[..., -1:] - g_cum)[..., None]
    g_tot = jnp.exp(g_cum[..., -1])

    def step(s, xs):
        qg_c, w_c, u_c, attn_c, kt_c, gt_c = xs
        v_new = u_c - jnp.einsum('bhld,bhde->bhle', w_c, s)
        out = jnp.einsum('bhld,bhde->bhle', qg_c, s) + jnp.einsum('bhls,bhse->bhle', attn_c, v_new)
        s = gt_c[..., None, None] * s + jnp.einsum('bhld,bhle->bhde', kt_c, v_new)
        return s, out

    bsz, nh, _, _, dk = q.shape
    init = jnp.zeros((bsz, nh, dk, dv), jnp.float32)
    _, out = lax.scan(step, init, (_chunk_major(q_g), _chunk_major(w), _chunk_major(u),
                                   _chunk_major(attn), _chunk_major(k_tail), _chunk_major(g_tot)))
    return jnp.moveaxis(out, 0, 2)


def _causal_depthwise_conv(x, w):
    return lax.conv_general_dilated(
        x, w[:, None, :].astype(x.dtype), window_strides=(1,), padding=[(CONV_WIDTH - 1, 0)],
        dimension_numbers=('NWC', 'WIO', 'NWC'), feature_group_count=x.shape[-1])


def _mlstm_retention_mixer(h, positions, w_in, gate_bias, mlstm_norm, w_out):
    bsz, s, _ = h.shape
    f32 = jnp.float32
    proj = h @ w_in
    q_m, k_m, v_m, o_m, if_m, q_r, k_r, v_r, g_r = _split(proj, L0_SPLITS)
    q_m = q_m.reshape(bsz, s, MLSTM_HEADS, MLSTM_DQK).astype(f32) * (MLSTM_DQK ** -0.5)
    k_m = k_m.reshape(bsz, s, MLSTM_HEADS, MLSTM_DQK).astype(f32)
    v_m = v_m.reshape(bsz, s, MLSTM_HEADS, MLSTM_DV).astype(f32)
    gates = if_m.astype(f32) + gate_bias.astype(f32)
    gates = GATE_SOFTCAP * jnp.tanh(gates / GATE_SOFTCAP)
    log_i = gates[..., :MLSTM_HEADS]
    log_f = jax.nn.log_sigmoid(gates[..., MLSTM_HEADS:])
    h_m = _from_chunks(_mlstm_chunkwise(_to_chunks(q_m), _to_chunks(k_m), _to_chunks(v_m),
                                        _gate_chunks(log_i), _gate_chunks(log_f)))
    h_m = _head_rmsnorm(h_m) * mlstm_norm.astype(f32).reshape(MLSTM_HEADS, MLSTM_DV)
    h_m = jax.nn.sigmoid(o_m.astype(f32)) * h_m.reshape(bsz, s, MLSTM_V_W)
    q_r = _rotary(q_r.reshape(bsz, s, RET_HEADS, RET_DK).astype(f32), positions)
    k_r = _rotary(k_r.reshape(bsz, s, RET_HEADS, RET_DK).astype(f32), positions) * (RET_DK ** -0.5)
    v_r = v_r.reshape(bsz, s, RET_HEADS, RET_DV).astype(f32)
    log_gamma = jnp.log(1.0 - 2.0 ** (-5.0 - jnp.arange(RET_HEADS, dtype=f32)))
    y_r = _from_chunks(_retention_chunkwise(_to_chunks(q_r), _to_chunks(k_r), _to_chunks(v_r), log_gamma))
    y_r = _head_rmsnorm(y_r).reshape(bsz, s, RET_V_W) * jax.nn.silu(g_r.astype(f32))
    mixed = jnp.concatenate([h_m, y_r], axis=-1).astype(h.dtype)
    return mixed @ w_out


def _gated_deltanet_mixer(h, w_in, conv_w, a_log, dt_bias, norm_w, w_out):
    bsz, s, _ = h.shape
    f32 = jnp.float32
    proj = h @ w_in
    qkv, z, b_pre, a_pre = _split(proj, L1_SPLITS)
    qkv = jax.nn.silu(_causal_depthwise_conv(qkv, conv_w))
    q, k, v = _split(qkv, [GDN_QK_W, GDN_QK_W, GDN_V_W])
    rep = GDN_V_HEADS // GDN_QK_HEADS
    q = jnp.repeat(_l2norm(q.reshape(bsz, s, GDN_QK_HEADS, GDN_DK).astype(f32)), rep, axis=2) * (GDN_DK ** -0.5)
    k = jnp.repeat(_l2norm(k.reshape(bsz, s, GDN_QK_HEADS, GDN_DK).astype(f32)), rep, axis=2)
    v = v.reshape(bsz, s, GDN_V_HEADS, GDN_DV).astype(f32)
    beta = jax.nn.sigmoid(b_pre.astype(f32))
    log_alpha = -jnp.exp(a_log.astype(f32)) * jax.nn.softplus(a_pre.astype(f32) + dt_bias.astype(f32))
    o = _from_chunks(_gated_delta_chunkwise(_to_chunks(q), _to_chunks(k), _to_chunks(v),
                                            _gate_chunks(log_alpha), _gate_chunks(beta)))
    o = _head_rmsnorm(o) * norm_w.astype(f32)
    o = o.reshape(bsz, s, GDN_V_W) * jax.nn.silu(z.astype(f32))
    return o.astype(h.dtype) @ w_out


def _swiglu(h, w_gate, w_up, w_down):
    return (jax.nn.silu(h @ w_gate) * (h @ w_up)) @ w_down


def setup_inputs(seed: int = 0) -> dict:
    key = jax.random.key(seed)
    ks = jax.random.split(key, 32)
    f32 = jnp.float32

    def dense(k, fan_in, fan_out):
        return jax.random.normal(k, (fan_in, fan_out), f32) * (fan_in ** -0.5)

    def gain(k, n):
        return 1.0 + 0.02 * jax.random.normal(k, (n,), f32)

    x = jax.random.normal(ks[0], (BATCH, SEQ, D_MODEL), f32)
    positions = jnp.broadcast_to(jnp.arange(SEQ, dtype=jnp.int32)[None, :], (BATCH, SEQ))
    i_bias = 0.1 * jax.random.normal(ks[1], (MLSTM_HEADS,), f32)
    f_bias = jnp.linspace(3.0, 6.0, MLSTM_HEADS, dtype=f32) + 0.1 * jax.random.normal(ks[2], (MLSTM_HEADS,), f32)
    dt = jnp.exp(jax.random.uniform(ks[3], (GDN_V_HEADS,), f32, minval=math.log(1e-3), maxval=math.log(1e-1)))
    return {
        'x': x,
        'positions': positions,
        'l0_norm_mix': gain(ks[4], D_MODEL),
        'l0_w_in': dense(ks[5], D_MODEL, L0_IN_W),
        'l0_gate_bias': jnp.concatenate([i_bias, f_bias]),
        'l0_mlstm_norm': gain(ks[6], MLSTM_V_W),
        'l0_w_out': dense(ks[7], L0_MIX_W, D_MODEL),
        'l0_norm_ffn': gain(ks[8], D_MODEL),
        'l0_w_gate': dense(ks[9], D_MODEL, D_FF),
        'l0_w_up': dense(ks[10], D_MODEL, D_FF),
        'l0_w_down': dense(ks[11], D_FF, D_MODEL),
        'l1_norm_mix': gain(ks[12], D_MODEL),
        'l1_w_in': dense(ks[13], D_MODEL, L1_IN_W),
        'l1_conv_w': jax.random.normal(ks[14], (CONV_WIDTH, GDN_CONV_C), f32) * (CONV_WIDTH ** -0.5),
        'l1_a_log': jnp.log(jax.random.uniform(ks[15], (GDN_V_HEADS,), f32, minval=1.0, maxval=16.0)),
        'l1_dt_bias': dt + jnp.log(-jnp.expm1(-dt)),
        'l1_gdn_norm': gain(ks[16], GDN_DV),
        'l1_w_out': dense(ks[17], GDN_V_W, D_MODEL),
        'l1_norm_ffn': gain(ks[18], D_MODEL),
        'l1_w_gate': dense(ks[19], D_MODEL, D_FF),
        'l1_w_up': dense(ks[20], D_MODEL, D_FF),
        'l1_w_down': dense(ks[21], D_FF, D_MODEL),
        'final_norm': gain(ks[22], D_MODEL),
    }


def reference(x, positions, l0_norm_mix, l0_w_in, l0_gate_bias, l0_mlstm_norm, l0_w_out,
              l0_norm_ffn, l0_w_gate, l0_w_up, l0_w_down,
              l1_norm_mix, l1_w_in, l1_conv_w, l1_a_log, l1_dt_bias, l1_gdn_norm, l1_w_out,
              l1_norm_ffn, l1_w_gate, l1_w_up, l1_w_down, final_norm):
    norm_mix = [l0_norm_mix, l1_norm_mix]
    norm_ffn = [l0_norm_ffn, l1_norm_ffn]
    ffn = [(l0_w_gate, l0_w_up, l0_w_down), (l1_w_gate, l1_w_up, l1_w_down)]
    for layer in range(DEPTH):
        h = _rmsnorm(x, norm_mix[layer])
        if layer % 2 == 0:
            x = x + _mlstm_retention_mixer(h, positions, l0_w_in, l0_gate_bias, l0_mlstm_norm, l0_w_out)
        else:
            x = x + _gated_deltanet_mixer(h, l1_w_in, l1_conv_w, l1_a_log, l1_dt_bias, l1_gdn_norm, l1_w_out)
        h = _rmsnorm(x, norm_ffn[layer])
        x = x + _swiglu(h, *ffn[layer])
    return _rmsnorm(x, final_norm)
```

```python
import functools
import math

import jax
import jax.numpy as jnp
import numpy as np
from jax import lax
from jax.experimental import pallas as pl
from jax.experimental.pallas import tpu as pltpu

F32 = jnp.float32
BF16 = jnp.bfloat16

D_MODEL = 2048
NORM_EPS = 1e-6
MLSTM_HEADS = 4
MLSTM_DQK = 128
MLSTM_DV = 256
GATE_SOFTCAP = 15.0
RET_HEADS = 4
RET_DK = 256
RET_DV = 256
ROPE_BASE = 10000.0
GDN_QK_HEADS = 16
GDN_V_HEADS = 32
GDN_DK = 128
GDN_DV = 128
GDN_REP = GDN_V_HEADS // GDN_QK_HEADS
CONV_WIDTH = 4
D_FF = -(-8 * D_MODEL // (3 * 256)) * 256

MLSTM_QK_W = MLSTM_HEADS * MLSTM_DQK
MLSTM_V_W = MLSTM_HEADS * MLSTM_DV
RET_QK_W = RET_HEADS * RET_DK
RET_V_W = RET_HEADS * RET_DV
GDN_QK_W = GDN_QK_HEADS * GDN_DK
GDN_V_W = GDN_V_HEADS * GDN_DV

LANES = 128
SUBLANES = 8
GATE_ROWS = 128
VMEM_LIMIT = 56 * 1024 * 1024

L0_CHUNK = 256
GDN_CHUNK = 64
GDN_BLOCK = 256

NT_DIMS = (((1,), (1,)), ((), ()))
TN_DIMS = (((0,), (0,)), ((), ()))


def _dot(a, b):
    return jnp.dot(a, b, preferred_element_type=F32)


def _dot_nt(a, b):
    return lax.dot_general(a, b, NT_DIMS, preferred_element_type=F32)


def _dot_tn(a, b):
    return lax.dot_general(a, b, TN_DIMS, preferred_element_type=F32)


def _sigmoid(x):
    return 1.0 / (1.0 + jnp.exp(-x))


def _silu(x):
    return x * _sigmoid(x)


def _softplus(x):
    return jnp.maximum(x, 0.0) + jnp.log1p(jnp.exp(-jnp.abs(x)))


def _log_sigmoid(x):
    return jnp.minimum(x, 0.0) - jnp.log1p(jnp.exp(-jnp.abs(x)))


def _cumsum_lanes(x, seg):
    pos = lax.broadcasted_iota(jnp.int32, x.shape, 1) % seg
    s = 1
    while s < seg:
        x = x + jnp.where(pos >= s, pltpu.roll(x, s, 1), 0.0)
        s *= 2
    return x


def _rows_to_cols(r):
    t = r.shape[1]
    padded = jnp.concatenate([r, jnp.zeros((LANES - r.shape[0], t), F32)], axis=0)
    return padded.T


def _head_rms(y):
    return y * lax.rsqrt(jnp.mean(y * y, axis=-1, keepdims=True) + NORM_EPS)


def _norm_proj_kernel(x_ref, g_ref, w_ref, wgt_ref, o_ref, grow_ref, h_ref):
    @pl.when(pl.program_id(1) == 0)
    def _():
        x = x_ref[...]
        y = x * lax.rsqrt(jnp.mean(x * x, axis=-1, keepdims=True) + NORM_EPS)
        hb = (y * g_ref[...]).astype(BF16)
        h_ref[...] = hb
        grow_ref[...] = _dot_nt(wgt_ref[...], hb)

    o_ref[...] = _dot(h_ref[...], w_ref[...]).astype(o_ref.dtype)


def _norm_proj(x2d, gain, w, wgt, *, tm, tn):
    m, d = x2d.shape
    n = w.shape[1]
    assert m % tm == 0 and n % tn == 0
    return pl.pallas_call(
        _norm_proj_kernel,
        grid=(m // tm, n // tn),
        in_specs=[
            pl.BlockSpec((tm, d), lambda i, j: (i, 0)),
            pl.BlockSpec((1, d), lambda i, j: (0, 0)),
            pl.BlockSpec((d, tn), lambda i, j: (0, j)),
            pl.BlockSpec((GATE_ROWS, d), lambda i, j: (0, 0)),
        ],
        out_specs=[
            pl.BlockSpec((tm, tn), lambda i, j: (i, j)),
            pl.BlockSpec((GATE_ROWS, tm), lambda i, j: (0, i)),
        ],
        out_shape=[
            jax.ShapeDtypeStruct((m, n), BF16),
            jax.ShapeDtypeStruct((GATE_ROWS, m), F32),
        ],
        scratch_shapes=[pltpu.VMEM((tm, d), BF16)],
        compiler_params=pltpu.CompilerParams(
            dimension_semantics=("arbitrary", "arbitrary"), vmem_limit_bytes=VMEM_LIMIT),
        name="norm_proj",
    )(x2d, gain.reshape(1, d), w, wgt)


def _out_proj_kernel(x_ref, a_ref, w_ref, o_ref):
    o_ref[...] = x_ref[...] + _dot(a_ref[...], w_ref[...])


def _out_proj(x2d, a, w, *, tm, tn):
    m, d = x2d.shape
    k = a.shape[1]
    assert m % tm == 0 and d % tn == 0
    return pl.pallas_call(
        _out_proj_kernel,
        grid=(m // tm, d // tn),
        in_specs=[
            pl.BlockSpec((tm, tn), lambda i, j: (i, j)),
            pl.BlockSpec((tm, k), lambda i, j: (i, 0)),
            pl.BlockSpec((k, tn), lambda i, j: (0, j)),
        ],
        out_specs=pl.BlockSpec((tm, tn), lambda i, j: (i, j)),
        out_shape=jax.ShapeDtypeStruct((m, d), F32),
        compiler_params=pltpu.CompilerParams(
            dimension_semantics=("arbitrary", "arbitrary"), vmem_limit_bytes=VMEM_LIMIT),
        name="out_proj",
    )(x2d, a, w)


def _ffn_kernel(x_ref, g_ref, wg_ref, wu_ref, wd_ref, fg_ref, o_ref, h_ref, *, final_norm):
    j = pl.program_id(1)

    @pl.when(j == 0)
    def _():
        x = x_ref[...]
        y = x * lax.rsqrt(jnp.mean(x * x, axis=-1, keepdims=True) + NORM_EPS)
        h_ref[...] = (y * g_ref[...]).astype(BF16)

    hb = h_ref[...]
    a = _dot(hb, wg_ref[...])
    b = _dot(hb, wu_ref[...])
    p = _dot((_silu(a) * b).astype(BF16), wd_ref[...])

    @pl.when(j == 0)
    def _():
        o_ref[...] = x_ref[...] + p

    @pl.when(j > 0)
    def _():
        o_ref[...] += p

    if final_norm:
        @pl.when(j == pl.num_programs(1) - 1)
        def _():
            x = o_ref[...]
            y = x * lax.rsqrt(jnp.mean(x * x, axis=-1, keepdims=True) + NORM_EPS)
            o_ref[...] = y * fg_ref[...]


def _ffn(x2d, gain, wg, wu, wd, final_gain, *, tm, tf, final_norm):
    m, d = x2d.shape
    f = wg.shape[1]
    assert m % tm == 0 and f % tf == 0
    return pl.pallas_call(
        functools.partial(_ffn_kernel, final_norm=final_norm),
        grid=(m // tm, f // tf),
        in_specs=[
            pl.BlockSpec((tm, d), lambda i, j: (i, 0)),
            pl.BlockSpec((1, d), lambda i, j: (0, 0)),
            pl.BlockSpec((d, tf), lambda i, j: (0, j)),
            pl.BlockSpec((d, tf), lambda i, j: (0, j)),
            pl.BlockSpec((tf, d), lambda i, j: (j, 0)),
            pl.BlockSpec((1, d), lambda i, j: (0, 0)),
        ],
        out_specs=pl.BlockSpec((tm, d), lambda i, j: (i, 0)),
        out_shape=jax.ShapeDtypeStruct((m, d), F32),
        scratch_shapes=[pltpu.VMEM((tm, d), BF16)],
        compiler_params=pltpu.CompilerParams(
            dimension_semantics=("arbitrary", "arbitrary"), vmem_limit_bytes=VMEM_LIMIT),
        name="ffn_final" if final_norm else "ffn",
    )(x2d, gain.reshape(1, d), wg, wu, wd, final_gain.reshape(1, d))


def _mlstm_kernel(q_ref, k_ref, v_ref, og_ref, grow_ref, bias_ref, norm_ref, o_ref,
                  c_ref, n_ref, m_ref):
    L = q_ref.shape[0]
    nh = MLSTM_HEADS

    @pl.when(pl.program_id(1) == 0)
    def _():
        c_ref[...] = jnp.zeros_like(c_ref)
        n_ref[...] = jnp.zeros_like(n_ref)
        m_ref[...] = jnp.full_like(m_ref, -jnp.inf)

    g = grow_ref[...] + bias_ref[...]
    g = GATE_SOFTCAP * jnp.tanh(g / GATE_SOFTCAP)
    row = lax.broadcasted_iota(jnp.int32, g.shape, 0)
    bcum = _cumsum_lanes(_log_sigmoid(g), L)
    rows = jnp.where(row < nh, g, bcum)
    cols = _rows_to_cols(rows)

    li = lax.broadcasted_iota(jnp.int32, (L, L), 0)
    si = lax.broadcasted_iota(jnp.int32, (L, L), 1)
    causal = si <= li
    scale = MLSTM_DQK ** -0.5

    for h in range(nh):
        li_r = rows[h:h + 1, :]
        bc_r = rows[nh + h:nh + h + 1, :]
        li_c = cols[:, h:h + 1]
        bc_c = cols[:, nh + h:nh + h + 1]
        a_r = li_r - bc_r
        a_c = li_c - bc_c
        g_tot = bc_c[L - 1:L, :]
        m_prev = m_ref[h][:, 0:1]

        log_d = jnp.where(causal, bc_c + a_r, -jnp.inf)
        m_intra = jnp.max(log_d, axis=-1, keepdims=True)
        m_inter = bc_c + m_prev
        m_t = jnp.maximum(m_inter, m_intra)
        inter = jnp.exp(m_inter - m_t) * scale
        dmat = jnp.exp(log_d - m_t)

        q = q_ref[:, h * MLSTM_DQK:(h + 1) * MLSTM_DQK]
        k = k_ref[:, h * MLSTM_DQK:(h + 1) * MLSTM_DQK]
        v = v_ref[:, h * MLSTM_DV:(h + 1) * MLSTM_DV]
        c_mat = c_ref[h]
        n_vec = n_ref[h]

        s = _dot_nt(q, k) * scale * dmat
        num = _dot(s.astype(BF16), v) + inter * _dot(q, c_mat.astype(BF16))
        qn = jnp.sum(q.astype(F32) * n_vec, axis=-1, keepdims=True)
        den = jnp.sum(s, axis=-1, keepdims=True) + inter * qn
        hout = num * (1.0 / jnp.maximum(jnp.abs(den), jnp.exp(-m_t)))

        y = _head_rms(hout) * norm_ref[:, h * MLSTM_DV:(h + 1) * MLSTM_DV]
        og = og_ref[:, h * MLSTM_DV:(h + 1) * MLSTM_DV].astype(F32)
        o_ref[:, h * MLSTM_DV:(h + 1) * MLSTM_DV] = (_sigmoid(og) * y).astype(o_ref.dtype)

        log_w = g_tot + a_c
        m_w = jnp.max(log_w, axis=0, keepdims=True)
        m_new = jnp.maximum(g_tot + m_prev, m_w)
        decay = jnp.exp(g_tot + m_prev - m_new)
        kw = k.astype(F32) * jnp.exp(log_w - m_new)
        c_ref[h] = decay * c_mat + _dot_tn(kw.astype(BF16), v)
        n_ref[h] = decay * n_vec + jnp.sum(kw, axis=0, keepdims=True)
        m_ref[h] = jnp.broadcast_to(m_new, (1, LANES))


def _mlstm(proj, grow, gate_bias, norm_w, *, batch, seq):
    m = proj.shape[0]
    L = L0_CHUNK
    nc = seq // L
    tok = lambda b, c: b * nc + c
    return pl.pallas_call(
        _mlstm_kernel,
        grid=(batch, nc),
        in_specs=[
            pl.BlockSpec((L, MLSTM_QK_W), lambda b, c: (tok(b, c), 0)),
            pl.BlockSpec((L, MLSTM_QK_W), lambda b, c: (tok(b, c), 1)),
            pl.BlockSpec((L, MLSTM_V_W), lambda b, c: (tok(b, c), 1)),
            pl.BlockSpec((L, MLSTM_V_W), lambda b, c: (tok(b, c), 2)),
            pl.BlockSpec((SUBLANES, L), lambda b, c: (0, tok(b, c))),
            pl.BlockSpec((SUBLANES, 1), lambda b, c: (0, 0)),
            pl.BlockSpec((1, MLSTM_V_W), lambda b, c: (0, 0)),
        ],
        out_specs=pl.BlockSpec((L, MLSTM_V_W), lambda b, c: (tok(b, c), 0)),
        out_shape=jax.ShapeDtypeStruct((m, MLSTM_V_W), BF16),
        scratch_shapes=[
            pltpu.VMEM((MLSTM_HEADS, MLSTM_DQK, MLSTM_DV), F32),
            pltpu.VMEM((MLSTM_HEADS, 1, MLSTM_DQK), F32),
            pltpu.VMEM((MLSTM_HEADS, 1, LANES), F32),
        ],
        compiler_params=pltpu.CompilerParams(
            dimension_semantics=("arbitrary", "arbitrary"), vmem_limit_bytes=VMEM_LIMIT),
        name="mlstm",
    )(proj, proj, proj, proj, grow, gate_bias.reshape(2 * MLSTM_HEADS, 1), norm_w.reshape(1, MLSTM_V_W))


def _ret_log_gamma(h):
    return math.log(1.0 - 2.0 ** (-5.0 - h))


def _retention_kernel(q_ref, k_ref, v_ref, g_ref, pos_ref, invf_ref, o_ref, r_ref, d_ref):
    L = q_ref.shape[0]
    half = RET_DK // 2
    scale = RET_DK ** -0.5

    @pl.when((pl.program_id(0) == 0) & (pl.program_id(1) == 0))
    def _():
        li = lax.broadcasted_iota(jnp.int32, (L, L), 0)
        si = lax.broadcasted_iota(jnp.int32, (L, L), 1)
        rel = (li - si).astype(F32)
        for h in range(RET_HEADS):
            d_ref[h] = jnp.where(si <= li, jnp.exp(rel * _ret_log_gamma(h)), 0.0) * scale

    @pl.when(pl.program_id(1) == 0)
    def _():
        r_ref[...] = jnp.zeros_like(r_ref)

    ang_t = invf_ref[...] * pos_ref[...].astype(F32)
    cos = jnp.cos(ang_t).T
    sin = jnp.sin(ang_t).T
    p_c = lax.broadcasted_iota(jnp.int32, (L, 1), 0).astype(F32)

    def rot(t):
        t1 = t[:, :half].astype(F32)
        t2 = t[:, half:].astype(F32)
        return jnp.concatenate([t1 * cos - t2 * sin, t1 * sin + t2 * cos], axis=1)

    for h in range(RET_HEADS):
        lg = _ret_log_gamma(h)
        sl = slice(h * RET_DK, (h + 1) * RET_DK)
        qr = rot(q_ref[:, sl])
        kr = rot(k_ref[:, sl])
        v = v_ref[:, h * RET_DV:(h + 1) * RET_DV]
        qb = qr.astype(BF16)
        r_mat = r_ref[h]
        s = _dot_nt(qb, kr.astype(BF16)) * d_ref[h]
        intra = _dot(s.astype(BF16), v)
        inter = _dot(qb, r_mat.astype(BF16)) * (jnp.exp((p_c + 1.0) * lg) * scale)
        kw = kr * jnp.exp((L - 1.0 - p_c) * lg)
        r_ref[h] = math.exp(L * lg) * r_mat + _dot_tn(kw.astype(BF16), v)
        y = _head_rms(intra + inter) * _silu(g_ref[:, h * RET_DV:(h + 1) * RET_DV].astype(F32))
        o_ref[:, h * RET_DV:(h + 1) * RET_DV] = y.astype(o_ref.dtype)


def _retention(proj, positions, *, batch, seq):
    m = proj.shape[0]
    L = L0_CHUNK
    nc = seq // L
    tok = lambda b, c: b * nc + c
    half = RET_DK // 2
    inv_freq = ROPE_BASE ** (-jnp.arange(0, RET_DK, 2, dtype=F32) / RET_DK)
    base = (2 * MLSTM_QK_W + 2 * MLSTM_V_W) // RET_QK_W
    return pl.pallas_call(
        _retention_kernel,
        grid=(batch, nc),
        in_specs=[
            pl.BlockSpec((L, RET_QK_W), lambda b, c: (tok(b, c), base)),
            pl.BlockSpec((L, RET_QK_W), lambda b, c: (tok(b, c), base + 1)),
            pl.BlockSpec((L, RET_V_W), lambda b, c: (tok(b, c), base + 2)),
            pl.BlockSpec((L, RET_V_W), lambda b, c: (tok(b, c), base + 3)),
            pl.BlockSpec((None, 1, L), lambda b, c: (tok(b, c), 0, 0)),
            pl.BlockSpec((half, 1), lambda b, c: (0, 0)),
        ],
        out_specs=pl.BlockSpec((L, RET_V_W), lambda b, c: (tok(b, c), 0)),
        out_shape=jax.ShapeDtypeStruct((m, RET_V_W), BF16),
        scratch_shapes=[
            pltpu.VMEM((RET_HEADS, RET_DK, RET_DV), F32),
            pltpu.VMEM((RET_HEADS, L, L), F32),
        ],
        compiler_params=pltpu.CompilerParams(
            dimension_semantics=("arbitrary", "arbitrary"), vmem_limit_bytes=VMEM_LIMIT),
        name="retention",
    )(proj, proj, proj, proj, positions.reshape(batch * nc, 1, L), inv_freq.reshape(half, 1))


def _unit_lower_inverse(a):
    n = a.shape[0]
    ri = lax.broadcasted_iota(jnp.int32, (n, n), 0)
    ci = lax.broadcasted_iota(jnp.int32, (n, n), 1)
    p = -a
    t = jnp.where(ri == ci, 1.0, 0.0) + p
    step = 2
    while step < n:
        p = _dot(p.astype(BF16), p.astype(BF16))
        t = t + _dot(t.astype(BF16), p.astype(BF16))
        step *= 2
    return t


def _gdn_kernel(q_ref, k_ref, v_ref, z_ref, wq_ref, wk_ref, wv_ref, grow_ref, alog_ref, dtb_ref, norm_ref,
                o_ref, s_ref, xq_ref, xk_ref, xv_ref):
    T = q_ref.shape[0]
    C = GDN_CHUNK
    pad = SUBLANES

    @pl.when(pl.program_id(2) == 0)
    def _():
        s_ref[...] = jnp.zeros_like(s_ref)
        xq_ref[0:pad, :] = jnp.zeros((pad, xq_ref.shape[1]), F32)
        xk_ref[0:pad, :] = jnp.zeros((pad, xk_ref.shape[1]), F32)
        xv_ref[0:pad, :] = jnp.zeros((pad, xv_ref.shape[1]), F32)

    def conv_silu(x_ref, buf_ref, w_ref):
        buf_ref[pad:pad + T, :] = x_ref[...].astype(F32)
        y = w_ref[CONV_WIDTH - 1:CONV_WIDTH, :] * buf_ref[pad:pad + T, :]
        for j in range(1, CONV_WIDTH):
            y = y + w_ref[CONV_WIDTH - 1 - j:CONV_WIDTH - j, :] * buf_ref[pad - j:pad - j + T, :]
        buf_ref[0:pad, :] = buf_ref[T:T + pad, :]
        return _silu(y)

    qc = conv_silu(q_ref, xq_ref, wq_ref)
    kc = conv_silu(k_ref, xk_ref, wk_ref)
    vc = conv_silu(v_ref, xv_ref, wv_ref)
    qn = qc * (lax.rsqrt(jnp.sum(qc * qc, axis=-1, keepdims=True) + NORM_EPS) * (GDN_DK ** -0.5))
    kn = kc * lax.rsqrt(jnp.sum(kc * kc, axis=-1, keepdims=True) + NORM_EPS)
    qb = qn.astype(BF16)
    kb = kn.astype(BF16)

    g = grow_ref[...]
    row = lax.broadcasted_iota(jnp.int32, g.shape, 0)
    beta = _sigmoid(g)
    log_alpha = -jnp.exp(alog_ref[...]) * _softplus(g + dtb_ref[...])
    gcum = _cumsum_lanes(log_alpha, C)
    rows = jnp.where(row < GDN_REP, beta, gcum)
    cols = _rows_to_cols(rows)

    li = lax.broadcasted_iota(jnp.int32, (C, C), 0)
    si = lax.broadcasted_iota(jnp.int32, (C, C), 1)
    causal = si <= li
    strict = si < li

    for c in range(T // C):
        rs = slice(c * C, (c + 1) * C)
        q_c = qb[rs]
        k_c = kb[rs]
        kk = _dot_nt(k_c, k_c)
        qk = _dot_nt(q_c, k_c)
        for j in range(GDN_REP):
            beta_c = cols[rs, j:j + 1]
            gc_c = cols[rs, GDN_REP + j:GDN_REP + j + 1]
            gc_r = rows[GDN_REP + j:GDN_REP + j + 1, rs]
            g_last = gc_c[C - 1:C, :]
            decay = jnp.exp(jnp.where(causal, gc_c - gc_r, -jnp.inf))
            a = jnp.where(strict, beta_c * kk * decay, 0.0)
            t_inv = _unit_lower_inverse(a).astype(BF16)
            eg = jnp.exp(gc_c)
            v_c = vc[rs, j * GDN_DV:(j + 1) * GDN_DV]
            u = _dot(t_inv, (v_c * beta_c).astype(BF16))
            w = _dot(t_inv, (kn[rs] * (beta_c * eg)).astype(BF16))
            attn = jnp.where(causal, qk * decay, 0.0)
            s_mat = s_ref[j]
            sb = s_mat.astype(BF16)
            v_new = u - _dot(w.astype(BF16), sb)
            vb = v_new.astype(BF16)
            out = _dot((qn[rs] * eg).astype(BF16), sb) + _dot(attn.astype(BF16), vb)
            k_tail = kn[rs] * jnp.exp(g_last - gc_c)
            s_ref[j] = jnp.exp(g_last) * s_mat + _dot_tn(k_tail.astype(BF16), vb)
            zg = z_ref[rs, j * GDN_DV:(j + 1) * GDN_DV].astype(F32)
            y = _head_rms(out) * norm_ref[...] * _silu(zg)
            o_ref[rs, j * GDN_DV:(j + 1) * GDN_DV] = y.astype(o_ref.dtype)


def _gdn(proj, grow, conv_w, a_log, dt_bias, norm_w, *, batch, seq):
    m = proj.shape[0]
    T = GDN_BLOCK
    nt = seq // T
    hq_n = GDN_QK_HEADS
    vw = GDN_REP * GDN_DV
    tok = lambda b, h, t: b * nt + t
    kbase = GDN_QK_W // GDN_DK
    vbase = 2 * GDN_QK_W // vw
    zbase = (2 * GDN_QK_W + GDN_V_W) // vw

    def per_head_rows(p):
        t = jnp.zeros((hq_n, SUBLANES), F32)
        t = t.at[:, GDN_REP:2 * GDN_REP].set(p.astype(F32).reshape(hq_n, GDN_REP))
        return t.reshape(hq_n, SUBLANES, 1)

    return pl.pallas_call(
        _gdn_kernel,
        grid=(batch, hq_n, nt),
        in_specs=[
            pl.BlockSpec((T, GDN_DK), lambda b, h, t: (tok(b, h, t), h)),
            pl.BlockSpec((T, GDN_DK), lambda b, h, t: (tok(b, h, t), kbase + h)),
            pl.BlockSpec((T, vw), lambda b, h, t: (tok(b, h, t), vbase + h)),
            pl.BlockSpec((T, vw), lambda b, h, t: (tok(b, h, t), zbase + h)),
            pl.BlockSpec((CONV_WIDTH, GDN_DK), lambda b, h, t: (0, h)),
            pl.BlockSpec((CONV_WIDTH, GDN_DK), lambda b, h, t: (0, kbase + h)),
            pl.BlockSpec((CONV_WIDTH, vw), lambda b, h, t: (0, vbase + h)),
            pl.BlockSpec((SUBLANES, T), lambda b, h, t: (h, tok(b, h, t))),
            pl.BlockSpec((None, SUBLANES, 1), lambda b, h, t: (h, 0, 0)),
            pl.BlockSpec((None, SUBLANES, 1), lambda b, h, t: (h, 0, 0)),
            pl.BlockSpec((1, GDN_DV), lambda b, h, t: (0, 0)),
        ],
        out_specs=pl.BlockSpec((T, vw), lambda b, h, t: (tok(b, h, t), h)),
        out_shape=jax.ShapeDtypeStruct((m, GDN_V_W), BF16),
        scratch_shapes=[
            pltpu.VMEM((GDN_REP, GDN_DK, GDN_DV), F32),
            pltpu.VMEM((T + SUBLANES, GDN_DK), F32),
            pltpu.VMEM((T + SUBLANES, GDN_DK), F32),
            pltpu.VMEM((T + SUBLANES, vw), F32),
        ],
        compiler_params=pltpu.CompilerParams(
            dimension_semantics=("arbitrary", "arbitrary", "arbitrary"), vmem_limit_bytes=VMEM_LIMIT),
        name="gdn",
    )(proj, proj, proj, proj, conv_w, conv_w, conv_w, grow,
      per_head_rows(a_log), per_head_rows(dt_bias), norm_w.reshape(1, GDN_DV))


def _l0_weights(w_in):
    d = w_in.shape[0]
    o = np.cumsum([0, MLSTM_QK_W, MLSTM_QK_W, MLSTM_V_W, MLSTM_V_W, 2 * MLSTM_HEADS,
                   RET_QK_W, RET_QK_W, RET_V_W, RET_V_W])
    perm = np.concatenate([np.arange(0, RET_DK, 2), np.arange(1, RET_DK, 2)])
    head_perm = np.concatenate([h * RET_DK + perm for h in range(RET_HEADS)])
    cols = np.concatenate([
        np.arange(o[0], o[4]),
        o[5] + head_perm, o[6] + head_perm,
        np.arange(o[7], o[9]),
    ])
    w_main = w_in[:, cols].astype(BF16)
    gates = w_in[:, o[4]:o[5]].T
    wgt = jnp.zeros((GATE_ROWS, d), F32).at[:2 * MLSTM_HEADS].set(gates).astype(BF16)
    return w_main, wgt


def _l1_weights(w_in):
    d = w_in.shape[0]
    main_w = 2 * GDN_QK_W + 2 * GDN_V_W
    w_main = w_in[:, :main_w].astype(BF16)
    b_pre = w_in[:, main_w:main_w + GDN_V_HEADS].T.reshape(GDN_QK_HEADS, GDN_REP, d)
    a_pre = w_in[:, main_w + GDN_V_HEADS:main_w + 2 * GDN_V_HEADS].T.reshape(GDN_QK_HEADS, GDN_REP, d)
    wgt = jnp.zeros((GDN_QK_HEADS, SUBLANES, d), F32)
    wgt = wgt.at[:, 0:GDN_REP].set(b_pre).at[:, GDN_REP:2 * GDN_REP].set(a_pre)
    return w_main, wgt.reshape(GATE_ROWS, d).astype(BF16)


def kernel(x, positions, l0_norm_mix, l0_w_in, l0_gate_bias, l0_mlstm_norm, l0_w_out, l0_norm_ffn, l0_w_gate,
           l0_w_up, l0_w_down, l1_norm_mix, l1_w_in, l1_conv_w, l1_a_log, l1_dt_bias, l1_gdn_norm, l1_w_out,
           l1_norm_ffn, l1_w_gate, l1_w_up, l1_w_down, final_norm):
    batch, seq, d = x.shape
    m = batch * seq
    x0 = x.reshape(m, d)

    w0, wgt0 = _l0_weights(l0_w_in)
    proj0, grow0 = _norm_proj(x0, l0_norm_mix, w0, wgt0, tm=1024, tn=1024)
    h_m = _mlstm(proj0, grow0, l0_gate_bias, l0_mlstm_norm, batch=batch, seq=seq)
    y_r = _retention(proj0, positions, batch=batch, seq=seq)
    mixed = jnp.concatenate([h_m, y_r], axis=1)
    x1 = _out_proj(x0, mixed, l0_w_out.astype(BF16), tm=1024, tn=512)
    x1 = _ffn(x1, l0_norm_ffn, l0_w_gate.astype(BF16), l0_w_up.astype(BF16), l0_w_down.astype(BF16),
              final_norm, tm=512, tf=512, final_norm=False)

    w1, wgt1 = _l1_weights(l1_w_in)
    proj1, grow1 = _norm_proj(x1, l1_norm_mix, w1, wgt1, tm=1024, tn=1024)
    o = _gdn(proj1, grow1, l1_conv_w, l1_a_log, l1_dt_bias, l1_gdn_norm, batch=batch, seq=seq)
    x2 = _out_proj(x1, o, l1_w_out.astype(BF16), tm=1024, tn=512)
    out = _ffn(x2, l1_norm_ffn, l1_w_gate.astype(BF16), l1_w_up.astype(BF16), l1_w_down.astype(BF16),
               final_norm, tm=512, tf=512, final_norm=True)
    return out.reshape(batch, seq, d)
```

```python
import functools
import math

import jax
import jax.numpy as jnp
import numpy as np
from jax import lax
from jax.experimental import pallas as pl
from jax.experimental.pallas import tpu as pltpu

F32 = jnp.float32
BF16 = jnp.bfloat16

D_MODEL = 2048
NORM_EPS = 1e-6
MLSTM_HEADS = 4
MLSTM_DQK = 128
MLSTM_DV = 256
GATE_SOFTCAP = 15.0
RET_HEADS = 4
RET_DK = 256
RET_DV = 256
ROPE_BASE = 10000.0
GDN_QK_HEADS = 16
GDN_V_HEADS = 32
GDN_DK = 128
GDN_DV = 128
GDN_REP = GDN_V_HEADS // GDN_QK_HEADS
CONV_WIDTH = 4
D_FF = -(-8 * D_MODEL // (3 * 256)) * 256

MLSTM_QK_W = MLSTM_HEADS * MLSTM_DQK
MLSTM_V_W = MLSTM_HEADS * MLSTM_DV
RET_QK_W = RET_HEADS * RET_DK
RET_V_W = RET_HEADS * RET_DV
GDN_QK_W = GDN_QK_HEADS * GDN_DK
GDN_V_W = GDN_V_HEADS * GDN_DV

LANES = 128
SUBLANES = 8
GATE_ROWS = 128
VMEM_LIMIT = 56 * 1024 * 1024

L0_CHUNK = 256
GDN_CHUNK = 64
GDN_BLOCK = 256
GDN_GROUP = 4

NT_DIMS = (((1,), (1,)), ((), ()))
TN_DIMS = (((0,), (0,)), ((), ()))


def _dot(a, b):
    return jnp.dot(a, b, preferred_element_type=F32)


def _dot_nt(a, b):
    return lax.dot_general(a, b, NT_DIMS, preferred_element_type=F32)


def _dot_tn(a, b):
    return lax.dot_general(a, b, TN_DIMS, preferred_element_type=F32)


def _sigmoid(x):
    return 1.0 / (1.0 + jnp.exp(-x))


def _silu(x):
    return x * _sigmoid(x)


def _softplus(x):
    return jnp.maximum(x, 0.0) + jnp.log1p(jnp.exp(-jnp.abs(x)))


def _log_sigmoid(x):
    return jnp.minimum(x, 0.0) - jnp.log1p(jnp.exp(-jnp.abs(x)))


def _cumsum_lanes(x, seg):
    pos = lax.broadcasted_iota(jnp.int32, x.shape, 1) % seg
    s = 1
    while s < seg:
        x = x + jnp.where(pos >= s, pltpu.roll(x, s, 1), 0.0)
        s *= 2
    return x


def _rows_to_cols(r):
    t = r.shape[1]
    if r.shape[0] < LANES:
        r = jnp.concatenate([r, jnp.zeros((LANES - r.shape[0], t), F32)], axis=0)
    return r.T


def _head_rms(y):
    return y * lax.rsqrt(jnp.mean(y * y, axis=-1, keepdims=True) + NORM_EPS)


def _norm_proj_kernel(x_ref, g_ref, w_ref, wgt_ref, o_ref, grow_ref, h_ref):
    @pl.when(pl.program_id(1) == 0)
    def _():
        x = x_ref[...]
        y = x * lax.rsqrt(jnp.mean(x * x, axis=-1, keepdims=True) + NORM_EPS)
        hb = (y * g_ref[...]).astype(BF16)
        h_ref[...] = hb
        grow_ref[...] = _dot_nt(wgt_ref[...], hb)

    o_ref[...] = _dot(h_ref[...], w_ref[...]).astype(o_ref.dtype)


def _norm_proj(x2d, gain, w, wgt, *, tm, tn):
    m, d = x2d.shape
    n = w.shape[1]
    assert m % tm == 0 and n % tn == 0
    return pl.pallas_call(
        _norm_proj_kernel,
        grid=(m // tm, n // tn),
        in_specs=[
            pl.BlockSpec((tm, d), lambda i, j: (i, 0)),
            pl.BlockSpec((1, d), lambda i, j: (0, 0)),
            pl.BlockSpec((d, tn), lambda i, j: (0, j)),
            pl.BlockSpec((GATE_ROWS, d), lambda i, j: (0, 0)),
        ],
        out_specs=[
            pl.BlockSpec((tm, tn), lambda i, j: (i, j)),
            pl.BlockSpec((GATE_ROWS, tm), lambda i, j: (0, i)),
        ],
        out_shape=[
            jax.ShapeDtypeStruct((m, n), BF16),
            jax.ShapeDtypeStruct((GATE_ROWS, m), F32),
        ],
        scratch_shapes=[pltpu.VMEM((tm, d), BF16)],
        compiler_params=pltpu.CompilerParams(
            dimension_semantics=("arbitrary", "arbitrary"), vmem_limit_bytes=VMEM_LIMIT),
        name="norm_proj",
    )(x2d, gain.reshape(1, d), w, wgt)


def _out_proj_kernel(x_ref, *refs):
    o_ref = refs[-1]
    n_in = (len(refs) - 1) // 2
    acc = x_ref[...]
    for i in range(n_in):
        acc = acc + _dot(refs[i][...], refs[n_in + i][...])
    o_ref[...] = acc


def _out_proj(x2d, acts, weights, *, tm, tn):
    m, d = x2d.shape
    assert m % tm == 0 and d % tn == 0
    return pl.pallas_call(
        _out_proj_kernel,
        grid=(m // tm, d // tn),
        in_specs=(
            [pl.BlockSpec((tm, tn), lambda i, j: (i, j))]
            + [pl.BlockSpec((tm, a.shape[1]), lambda i, j: (i, 0)) for a in acts]
            + [pl.BlockSpec((w.shape[0], tn), lambda i, j: (0, j)) for w in weights]
        ),
        out_specs=pl.BlockSpec((tm, tn), lambda i, j: (i, j)),
        out_shape=jax.ShapeDtypeStruct((m, d), F32),
        compiler_params=pltpu.CompilerParams(
            dimension_semantics=("arbitrary", "arbitrary"), vmem_limit_bytes=VMEM_LIMIT),
        name="out_proj",
    )(x2d, *acts, *weights)


def _ffn_kernel(x_ref, g_ref, wg_ref, wu_ref, wd_ref, fg_ref, o_ref, h_ref, *, final_norm):
    j = pl.program_id(1)

    @pl.when(j == 0)
    def _():
        x = x_ref[...]
        y = x * lax.rsqrt(jnp.mean(x * x, axis=-1, keepdims=True) + NORM_EPS)
        h_ref[...] = (y * g_ref[...]).astype(BF16)

    hb = h_ref[...]
    a = _dot(hb, wg_ref[...])
    b = _dot(hb, wu_ref[...])
    p = _dot((_silu(a) * b).astype(BF16), wd_ref[...])

    @pl.when(j == 0)
    def _():
        o_ref[...] = x_ref[...] + p

    @pl.when(j > 0)
    def _():
        o_ref[...] += p

    if final_norm:
        @pl.when(j == pl.num_programs(1) - 1)
        def _():
            x = o_ref[...]
            y = x * lax.rsqrt(jnp.mean(x * x, axis=-1, keepdims=True) + NORM_EPS)
            o_ref[...] = y * fg_ref[...]


def _ffn(x2d, gain, wg, wu, wd, final_gain, *, tm, tf, final_norm):
    m, d = x2d.shape
    f = wg.shape[1]
    assert m % tm == 0 and f % tf == 0
    return pl.pallas_call(
        functools.partial(_ffn_kernel, final_norm=final_norm),
        grid=(m // tm, f // tf),
        in_specs=[
            pl.BlockSpec((tm, d), lambda i, j: (i, 0)),
            pl.BlockSpec((1, d), lambda i, j: (0, 0)),
            pl.BlockSpec((d, tf), lambda i, j: (0, j)),
            pl.BlockSpec((d, tf), lambda i, j: (0, j)),
            pl.BlockSpec((tf, d), lambda i, j: (j, 0)),
            pl.BlockSpec((1, d), lambda i, j: (0, 0)),
        ],
        out_specs=pl.BlockSpec((tm, d), lambda i, j: (i, 0)),
        out_shape=jax.ShapeDtypeStruct((m, d), F32),
        scratch_shapes=[pltpu.VMEM((tm, d), BF16)],
        compiler_params=pltpu.CompilerParams(
            dimension_semantics=("arbitrary", "arbitrary"), vmem_limit_bytes=VMEM_LIMIT),
        name="ffn_final" if final_norm else "ffn",
    )(x2d, gain.reshape(1, d), wg, wu, wd, final_gain.reshape(1, d))


def _mlstm_kernel(q_ref, k_ref, v_ref, og_ref, grow_ref, bias_ref, norm_ref, o_ref,
                  c_ref, n_ref, m_ref):
    L = q_ref.shape[0]
    nh = MLSTM_HEADS

    @pl.when(pl.program_id(1) == 0)
    def _():
        c_ref[...] = jnp.zeros_like(c_ref)
        n_ref[...] = jnp.zeros_like(n_ref)
        m_ref[...] = jnp.full_like(m_ref, -jnp.inf)

    g = grow_ref[...] + bias_ref[...]
    g = GATE_SOFTCAP * jnp.tanh(g / GATE_SOFTCAP)
    row = lax.broadcasted_iota(jnp.int32, g.shape, 0)
    bcum = _cumsum_lanes(_log_sigmoid(g), L)
    rows = jnp.where(row < nh, g, bcum)
    cols = _rows_to_cols(rows)

    li = lax.broadcasted_iota(jnp.int32, (L, L), 0)
    si = lax.broadcasted_iota(jnp.int32, (L, L), 1)
    causal = si <= li
    scale = MLSTM_DQK ** -0.5

    for h in range(nh):
        li_r = rows[h:h + 1, :]
        bc_r = rows[nh + h:nh + h + 1, :]
        li_c = cols[:, h:h + 1]
        bc_c = cols[:, nh + h:nh + h + 1]
        a_r = li_r - bc_r
        a_c = li_c - bc_c
        g_tot = bc_c[L - 1:L, :]
        m_prev = m_ref[h][:, 0:1]

        log_d = jnp.where(causal, bc_c + a_r, -jnp.inf)
        m_intra = jnp.max(log_d, axis=-1, keepdims=True)
        m_inter = bc_c + m_prev
        m_t = jnp.maximum(m_inter, m_intra)
        inter = jnp.exp(m_inter - m_t) * scale
        dmat = jnp.exp(log_d - m_t)

        q = q_ref[:, h * MLSTM_DQK:(h + 1) * MLSTM_DQK]
        k = k_ref[:, h * MLSTM_DQK:(h + 1) * MLSTM_DQK]
        v = v_ref[:, h * MLSTM_DV:(h + 1) * MLSTM_DV]
        c_mat = c_ref[h]
        n_vec = n_ref[h]

        s = _dot_nt(q, k) * scale * dmat
        num = _dot(s.astype(BF16), v) + inter * _dot(q, c_mat.astype(BF16))
        qn = jnp.sum(q.astype(F32) * n_vec, axis=-1, keepdims=True)
        den = jnp.sum(s, axis=-1, keepdims=True) + inter * qn
        hout = num * (1.0 / jnp.maximum(jnp.abs(den), jnp.exp(-m_t)))

        y = _head_rms(hout) * norm_ref[:, h * MLSTM_DV:(h + 1) * MLSTM_DV]
        og = og_ref[:, h * MLSTM_DV:(h + 1) * MLSTM_DV].astype(F32)
        o_ref[:, h * MLSTM_DV:(h + 1) * MLSTM_DV] = (_sigmoid(og) * y).astype(o_ref.dtype)

        log_w = g_tot + a_c
        m_w = jnp.max(log_w, axis=0, keepdims=True)
        m_new = jnp.maximum(g_tot + m_prev, m_w)
        decay = jnp.exp(g_tot + m_prev - m_new)
        kw = k.astype(F32) * jnp.exp(log_w - m_new)
        c_ref[h] = decay * c_mat + _dot_tn(kw.astype(BF16), v)
        n_ref[h] = decay * n_vec + jnp.sum(kw, axis=0, keepdims=True)
        m_ref[h] = jnp.broadcast_to(m_new, (1, LANES))


def _mlstm(proj, grow, gate_bias, norm_w, *, batch, seq):
    m = proj.shape[0]
    L = L0_CHUNK
    nc = seq // L
    tok = lambda b, c: b * nc + c
    return pl.pallas_call(
        _mlstm_kernel,
        grid=(batch, nc),
        in_specs=[
            pl.BlockSpec((L, MLSTM_QK_W), lambda b, c: (tok(b, c), 0)),
            pl.BlockSpec((L, MLSTM_QK_W), lambda b, c: (tok(b, c), 1)),
            pl.BlockSpec((L, MLSTM_V_W), lambda b, c: (tok(b, c), 1)),
            pl.BlockSpec((L, MLSTM_V_W), lambda b, c: (tok(b, c), 2)),
            pl.BlockSpec((SUBLANES, L), lambda b, c: (0, tok(b, c))),
            pl.BlockSpec((SUBLANES, 1), lambda b, c: (0, 0)),
            pl.BlockSpec((1, MLSTM_V_W), lambda b, c: (0, 0)),
        ],
        out_specs=pl.BlockSpec((L, MLSTM_V_W), lambda b, c: (tok(b, c), 0)),
        out_shape=jax.ShapeDtypeStruct((m, MLSTM_V_W), BF16),
        scratch_shapes=[
            pltpu.VMEM((MLSTM_HEADS, MLSTM_DQK, MLSTM_DV), F32),
            pltpu.VMEM((MLSTM_HEADS, 1, MLSTM_DQK), F32),
            pltpu.VMEM((MLSTM_HEADS, 1, LANES), F32),
        ],
        compiler_params=pltpu.CompilerParams(
            dimension_semantics=("arbitrary", "arbitrary"), vmem_limit_bytes=VMEM_LIMIT),
        name="mlstm",
    )(proj, proj, proj, proj, grow, gate_bias.reshape(2 * MLSTM_HEADS, 1), norm_w.reshape(1, MLSTM_V_W))


def _ret_log_gamma(h):
    return math.log(1.0 - 2.0 ** (-5.0 - h))


def _retention_kernel(q_ref, k_ref, v_ref, g_ref, pos_ref, invf_ref, o_ref, r_ref, d_ref):
    L = q_ref.shape[0]
    half = RET_DK // 2
    scale = RET_DK ** -0.5

    @pl.when((pl.program_id(0) == 0) & (pl.program_id(1) == 0))
    def _():
        li = lax.broadcasted_iota(jnp.int32, (L, L), 0)
        si = lax.broadcasted_iota(jnp.int32, (L, L), 1)
        rel = (li - si).astype(F32)
        for h in range(RET_HEADS):
            d_ref[h] = jnp.where(si <= li, jnp.exp(rel * _ret_log_gamma(h)), 0.0) * scale

    @pl.when(pl.program_id(1) == 0)
    def _():
        r_ref[...] = jnp.zeros_like(r_ref)

    ang_t = invf_ref[...] * pos_ref[...].astype(F32)
    cos = jnp.cos(ang_t).T
    sin = jnp.sin(ang_t).T
    p_c = lax.broadcasted_iota(jnp.int32, (L, 1), 0).astype(F32)

    def rot(t):
        t1 = t[:, :half].astype(F32)
        t2 = t[:, half:].astype(F32)
        return jnp.concatenate([t1 * cos - t2 * sin, t1 * sin + t2 * cos], axis=1)

    for h in range(RET_HEADS):
        lg = _ret_log_gamma(h)
        sl = slice(h * RET_DK, (h + 1) * RET_DK)
        qr = rot(q_ref[:, sl])
        kr = rot(k_ref[:, sl])
        v = v_ref[:, h * RET_DV:(h + 1) * RET_DV]
        qb = qr.astype(BF16)
        r_mat = r_ref[h]
        s = _dot_nt(qb, kr.astype(BF16)) * d_ref[h]
        intra = _dot(s.astype(BF16), v)
        inter = _dot(qb, r_mat.astype(BF16)) * (jnp.exp((p_c + 1.0) * lg) * scale)
        kw = kr * jnp.exp((L - 1.0 - p_c) * lg)
        r_ref[h] = math.exp(L * lg) * r_mat + _dot_tn(kw.astype(BF16), v)
        y = _head_rms(intra + inter) * _silu(g_ref[:, h * RET_DV:(h + 1) * RET_DV].astype(F32))
        o_ref[:, h * RET_DV:(h + 1) * RET_DV] = y.astype(o_ref.dtype)


def _retention(proj, positions, *, batch, seq):
    m = proj.shape[0]
    L = L0_CHUNK
    nc = seq // L
    tok = lambda b, c: b * nc + c
    half = RET_DK // 2
    inv_freq = ROPE_BASE ** (-jnp.arange(0, RET_DK, 2, dtype=F32) / RET_DK)
    base = (2 * MLSTM_QK_W + 2 * MLSTM_V_W) // RET_QK_W
    return pl.pallas_call(
        _retention_kernel,
        grid=(batch, nc),
        in_specs=[
            pl.BlockSpec((L, RET_QK_W), lambda b, c: (tok(b, c), base)),
            pl.BlockSpec((L, RET_QK_W), lambda b, c: (tok(b, c), base + 1)),
            pl.BlockSpec((L, RET_V_W), lambda b, c: (tok(b, c), base + 2)),
            pl.BlockSpec((L, RET_V_W), lambda b, c: (tok(b, c), base + 3)),
            pl.BlockSpec((None, 1, L), lambda b, c: (tok(b, c), 0, 0)),
            pl.BlockSpec((half, 1), lambda b, c: (0, 0)),
        ],
        out_specs=pl.BlockSpec((L, RET_V_W), lambda b, c: (tok(b, c), 0)),
        out_shape=jax.ShapeDtypeStruct((m, RET_V_W), BF16),
        scratch_shapes=[
            pltpu.VMEM((RET_HEADS, RET_DK, RET_DV), F32),
            pltpu.VMEM((RET_HEADS, L, L), F32),
        ],
        compiler_params=pltpu.CompilerParams(
            dimension_semantics=("arbitrary", "arbitrary"), vmem_limit_bytes=VMEM_LIMIT),
        name="retention",
    )(proj, proj, proj, proj, positions.reshape(batch * nc, 1, L), inv_freq.reshape(half, 1))


def _gdn_kernel(q_ref, k_ref, v_ref, z_ref, wq_ref, wk_ref, wv_ref, grow_ref, alog_ref, dtb_ref, norm_ref,
                o_ref, s_ref, xq_ref, xk_ref, xv_ref):
    T = q_ref.shape[0]
    C = GDN_CHUNK
    R = GDN_REP * C
    nch = T // C
    pad = SUBLANES
    assert GDN_REP == 2 and R == LANES

    @pl.when(pl.program_id(2) == 0)
    def _():
        s_ref[...] = jnp.zeros_like(s_ref)
        xq_ref[0:pad, :] = jnp.zeros((pad, xq_ref.shape[1]), F32)
        xk_ref[0:pad, :] = jnp.zeros((pad, xk_ref.shape[1]), F32)
        xv_ref[0:pad, :] = jnp.zeros((pad, xv_ref.shape[1]), F32)

    def conv_silu(x_ref, buf_ref, w_ref):
        buf_ref[pad:pad + T, :] = x_ref[...].astype(F32)
        y = w_ref[CONV_WIDTH - 1:CONV_WIDTH, :] * buf_ref[pad:pad + T, :]
        for j in range(1, CONV_WIDTH):
            y = y + w_ref[CONV_WIDTH - 1 - j:CONV_WIDTH - j, :] * buf_ref[pad - j:pad - j + T, :]
        buf_ref[0:pad, :] = buf_ref[T:T + pad, :]
        return _silu(y)

    qc = conv_silu(q_ref, xq_ref, wq_ref)
    kc = conv_silu(k_ref, xk_ref, wk_ref)
    vc = conv_silu(v_ref, xv_ref, wv_ref)

    g = grow_ref[...]
    row = lax.broadcasted_iota(jnp.int32, g.shape, 0) % SUBLANES
    beta = _sigmoid(g)
    log_alpha = -jnp.exp(alog_ref[...]) * _softplus(g + dtb_ref[...])
    gcum = _cumsum_lanes(log_alpha, C)
    rows = jnp.where(row < GDN_REP, beta, gcum)
    cols = _rows_to_cols(rows)

    ri = lax.broadcasted_iota(jnp.int32, (R, R), 0)
    ci = lax.broadcasted_iota(jnp.int32, (R, R), 1)
    same = (ri // C) == (ci // C)
    causal = same & (ci <= ri)
    strict = same & (ci < ri)
    eye = jnp.where(ri == ci, 1.0, 0.0)
    top = lax.broadcasted_iota(jnp.int32, (R, GDN_DV), 0) < C

    def stack2(a, b):
        return jnp.concatenate([a, b], axis=0)

    pairs = [(c, h) for c in range(nch) for h in range(GDN_GROUP)]
    pre = {}
    for (c, h) in pairs:
        rs = slice(c * C, (c + 1) * C)
        ks = slice(h * GDN_DK, (h + 1) * GDN_DK)
        q_c = qc[rs, ks]
        k_c = kc[rs, ks]
        q_c = q_c * (lax.rsqrt(jnp.sum(q_c * q_c, axis=-1, keepdims=True) + NORM_EPS) * (GDN_DK ** -0.5))
        k_c = k_c * lax.rsqrt(jnp.sum(k_c * k_c, axis=-1, keepdims=True) + NORM_EPS)
        k2 = stack2(k_c, k_c)
        q2 = stack2(q_c, q_c)
        k2b = k2.astype(BF16)
        base = h * SUBLANES
        beta2 = stack2(cols[rs, base:base + 1], cols[rs, base + 1:base + 2])
        gc2 = stack2(cols[rs, base + GDN_REP:base + GDN_REP + 1], cols[rs, base + GDN_REP + 1:base + GDN_REP + 2])
        gc2_r = jnp.concatenate([rows[base + GDN_REP:base + GDN_REP + 1, rs],
                                 rows[base + GDN_REP + 1:base + GDN_REP + 2, rs]], axis=1)
        gl0 = gc2[C - 1:C, :]
        gl1 = gc2[R - 1:R, :]
        gl2 = jnp.where(top[:, 0:1], gl0, gl1)
        decay = jnp.exp(jnp.where(causal, gc2 - gc2_r, -jnp.inf))
        kk = _dot_nt(k2b, k2b)
        qk = _dot_nt(q2.astype(BF16), k2b)
        eg2 = jnp.exp(gc2)
        v2 = stack2(vc[rs, (2 * h) * GDN_DV:(2 * h + 1) * GDN_DV], vc[rs, (2 * h + 1) * GDN_DV:(2 * h + 2) * GDN_DV])
        pre[(c, h)] = dict(
            a=jnp.where(strict, beta2 * kk * decay, 0.0),
            attn=jnp.where(causal, qk * decay, 0.0).astype(BF16),
            rhs=jnp.concatenate([v2 * beta2, k2 * (beta2 * eg2)], axis=1).astype(BF16),
            qg=(q2 * eg2).astype(BF16),
            kt=(k2 * jnp.exp(gl2 - gc2)).astype(BF16),
            gt=jnp.concatenate([jnp.broadcast_to(jnp.exp(gl0), (1, GDN_DV)),
                                jnp.broadcast_to(jnp.exp(gl1), (1, GDN_DV))], axis=1),
        )

    p = {key: -pre[key]["a"] for key in pairs}
    t = {key: eye + p[key] for key in pairs}
    step = 2
    while step < C:
        for key in pairs:
            pb = p[key].astype(BF16)
            p[key] = _dot(pb, pb)
        for key in pairs:
            t[key] = t[key] + _dot(t[key].astype(BF16), p[key].astype(BF16))
        step *= 2
    uw = {key: _dot(t[key].astype(BF16), pre[key]["rhs"]) for key in pairs}

    s2 = [s_ref[h] for h in range(GDN_GROUP)]
    for c in range(nch):
        rs = slice(c * C, (c + 1) * C)
        x1 = []
        for h in range(GDN_GROUP):
            w2b = uw[(c, h)][:, GDN_DV:].astype(BF16)
            lhs = jnp.concatenate([w2b, pre[(c, h)]["qg"]], axis=0)
            x1.append(_dot(lhs, s2[h].astype(BF16)))
        for h in range(GDN_GROUP):
            d = pre[(c, h)]
            ws = jnp.where(top, x1[h][:R, :GDN_DV], x1[h][:R, GDN_DV:])
            qs = jnp.where(top, x1[h][R:, :GDN_DV], x1[h][R:, GDN_DV:])
            vn = uw[(c, h)][:, :GDN_DV] - ws
            out = qs + _dot(d["attn"], vn.astype(BF16))
            vx = jnp.concatenate([jnp.where(top, vn, 0.0), jnp.where(top, 0.0, vn)], axis=1).astype(BF16)
            s2[h] = s2[h] * d["gt"] + _dot_tn(d["kt"], vx)
            z2 = stack2(z_ref[rs, (2 * h) * GDN_DV:(2 * h + 1) * GDN_DV],
                        z_ref[rs, (2 * h + 1) * GDN_DV:(2 * h + 2) * GDN_DV]).astype(F32)
            y = (_head_rms(out) * norm_ref[...] * _silu(z2)).astype(o_ref.dtype)
            o_ref[rs, (2 * h) * GDN_DV:(2 * h + 1) * GDN_DV] = y[:C]
            o_ref[rs, (2 * h + 1) * GDN_DV:(2 * h + 2) * GDN_DV] = y[C:]
    for h in range(GDN_GROUP):
        s_ref[h] = s2[h]


def _gdn(proj, grow, conv_w, a_log, dt_bias, norm_w, *, batch, seq):
    m = proj.shape[0]
    T = GDN_BLOCK
    G = GDN_GROUP
    nt = seq // T
    ng = GDN_QK_HEADS // G
    qw = G * GDN_DK
    vw = G * GDN_REP * GDN_DV
    tok = lambda b, h, t: b * nt + t
    kbase = GDN_QK_W // qw
    vbase = 2 * GDN_QK_W // vw
    zbase = (2 * GDN_QK_W + GDN_V_W) // vw

    def per_head_rows(p):
        t = jnp.zeros((GDN_QK_HEADS, SUBLANES), F32)
        t = t.at[:, GDN_REP:2 * GDN_REP].set(p.astype(F32).reshape(GDN_QK_HEADS, GDN_REP))
        return t.reshape(GDN_QK_HEADS * SUBLANES, 1)

    return pl.pallas_call(
        _gdn_kernel,
        grid=(batch, ng, nt),
        in_specs=[
            pl.BlockSpec((T, qw), lambda b, h, t: (tok(b, h, t), h)),
            pl.BlockSpec((T, qw), lambda b, h, t: (tok(b, h, t), kbase + h)),
            pl.BlockSpec((T, vw), lambda b, h, t: (tok(b, h, t), vbase + h)),
            pl.BlockSpec((T, vw), lambda b, h, t: (tok(b, h, t), zbase + h)),
            pl.BlockSpec((CONV_WIDTH, qw), lambda b, h, t: (0, h)),
            pl.BlockSpec((CONV_WIDTH, qw), lambda b, h, t: (0, kbase + h)),
            pl.BlockSpec((CONV_WIDTH, vw), lambda b, h, t: (0, vbase + h)),
            pl.BlockSpec((G * SUBLANES, T), lambda b, h, t: (h, tok(b, h, t))),
            pl.BlockSpec((G * SUBLANES, 1), lambda b, h, t: (h, 0)),
            pl.BlockSpec((G * SUBLANES, 1), lambda b, h, t: (h, 0)),
            pl.BlockSpec((1, GDN_DV), lambda b, h, t: (0, 0)),
        ],
        out_specs=pl.BlockSpec((T, vw), lambda b, h, t: (tok(b, h, t), h)),
        out_shape=jax.ShapeDtypeStruct((m, GDN_V_W), BF16),
        scratch_shapes=[
            pltpu.VMEM((G, GDN_DK, GDN_REP * GDN_DV), F32),
            pltpu.VMEM((T + SUBLANES, qw), F32),
            pltpu.VMEM((T + SUBLANES, qw), F32),
            pltpu.VMEM((T + SUBLANES, vw), F32),
        ],
        compiler_params=pltpu.CompilerParams(
            dimension_semantics=("arbitrary", "arbitrary", "arbitrary"), vmem_limit_bytes=VMEM_LIMIT),
        name="gdn",
    )(proj, proj, proj, proj, conv_w, conv_w, conv_w, grow,
      per_head_rows(a_log), per_head_rows(dt_bias), norm_w.reshape(1, GDN_DV))


def _l0_weights(w_in):
    d = w_in.shape[0]
    o = np.cumsum([0, MLSTM_QK_W, MLSTM_QK_W, MLSTM_V_W, MLSTM_V_W, 2 * MLSTM_HEADS,
                   RET_QK_W, RET_QK_W, RET_V_W, RET_V_W])
    perm = np.concatenate([np.arange(0, RET_DK, 2), np.arange(1, RET_DK, 2)])
    head_perm = np.concatenate([h * RET_DK + perm for h in range(RET_HEADS)])
    cols = np.concatenate([
        np.arange(o[0], o[4]),
        o[5] + head_perm, o[6] + head_perm,
        np.arange(o[7], o[9]),
    ])
    w_main = w_in[:, cols].astype(BF16)
    gates = w_in[:, o[4]:o[5]].T
    wgt = jnp.zeros((GATE_ROWS, d), F32).at[:2 * MLSTM_HEADS].set(gates).astype(BF16)
    return w_main, wgt


def _l1_weights(w_in):
    d = w_in.shape[0]
    main_w = 2 * GDN_QK_W + 2 * GDN_V_W
    w_main = w_in[:, :main_w].astype(BF16)
    b_pre = w_in[:, main_w:main_w + GDN_V_HEADS].T.reshape(GDN_QK_HEADS, GDN_REP, d)
    a_pre = w_in[:, main_w + GDN_V_HEADS:main_w + 2 * GDN_V_HEADS].T.reshape(GDN_QK_HEADS, GDN_REP, d)
    wgt = jnp.zeros((GDN_QK_HEADS, SUBLANES, d), F32)
    wgt = wgt.at[:, 0:GDN_REP].set(b_pre).at[:, GDN_REP:2 * GDN_REP].set(a_pre)
    return w_main, wgt.reshape(GATE_ROWS, d).astype(BF16)


def kernel(x, positions, l0_norm_mix, l0_w_in, l0_gate_bias, l0_mlstm_norm, l0_w_out, l0_norm_ffn, l0_w_gate,
           l0_w_up, l0_w_down, l1_norm_mix, l1_w_in, l1_conv_w, l1_a_log, l1_dt_bias, l1_gdn_norm, l1_w_out,
           l1_norm_ffn, l1_w_gate, l1_w_up, l1_w_down, final_norm):
    batch, seq, d = x.shape
    m = batch * seq
    x0 = x.reshape(m, d)

    w0, wgt0 = _l0_weights(l0_w_in)
    proj0, grow0 = _norm_proj(x0, l0_norm_mix, w0, wgt0, tm=1024, tn=1024)
    h_m = _mlstm(proj0, grow0, l0_gate_bias, l0_mlstm_norm, batch=batch, seq=seq)
    y_r = _retention(proj0, positions, batch=batch, seq=seq)
    wo0 = l0_w_out.astype(BF16)
    x1 = _out_proj(x0, [h_m, y_r], [wo0[:MLSTM_V_W], wo0[MLSTM_V_W:]], tm=1024, tn=512)
    x1 = _ffn(x1, l0_norm_ffn, l0_w_gate.astype(BF16), l0_w_up.astype(BF16), l0_w_down.astype(BF16),
              final_norm, tm=512, tf=512, final_norm=False)

    w1, wgt1 = _l1_weights(l1_w_in)
    proj1, grow1 = _norm_proj(x1, l1_norm_mix, w1, wgt1, tm=1024, tn=1024)
    o = _gdn(proj1, grow1, l1_conv_w, l1_a_log, l1_dt_bias, l1_gdn_norm, batch=batch, seq=seq)
    x2 = _out_proj(x1, [o], [l1_w_out.astype(BF16)], tm=1024, tn=512)
    out = _ffn(x2, l1_norm_ffn, l1_w_gate.astype(BF16), l1_w_up.astype(BF16), l1_w_down.astype(BF16),
               final_norm, tm=512, tf=512, final_norm=True)
    return out.reshape(batch, seq, d)
```

```python
import functools
import math

import jax
import jax.numpy as jnp
import numpy as np
from jax import lax
from jax.experimental import pallas as pl
from jax.experimental.pallas import tpu as pltpu

F32 = jnp.float32
BF16 = jnp.bfloat16

D_MODEL = 2048
NORM_EPS = 1e-6
MLSTM_HEADS = 4
MLSTM_DQK = 128
MLSTM_DV = 256
GATE_SOFTCAP = 15.0
RET_HEADS = 4
RET_DK = 256
RET_DV = 256
ROPE_BASE = 10000.0
GDN_QK_HEADS = 16
GDN_V_HEADS = 32
GDN_DK = 128
GDN_DV = 128
GDN_REP = GDN_V_HEADS // GDN_QK_HEADS
CONV_WIDTH = 4
D_FF = -(-8 * D_MODEL // (3 * 256)) * 256

MLSTM_QK_W = MLSTM_HEADS * MLSTM_DQK
MLSTM_V_W = MLSTM_HEADS * MLSTM_DV
RET_QK_W = RET_HEADS * RET_DK
RET_V_W = RET_HEADS * RET_DV
GDN_QK_W = GDN_QK_HEADS * GDN_DK
GDN_V_W = GDN_V_HEADS * GDN_DV

LANES = 128
SUBLANES = 8
MXU_WIDTH = 256
PROJ_PIECE = MXU_WIDTH
GATE_ROWS = 128
VMEM_LIMIT = 56 * 1024 * 1024

L0_CHUNK = 256
GDN_CHUNK = 64
GDN_BLOCK = 256
GDN_GROUP = 4
GDN_PHASE_CHUNKS = 2

NT_DIMS = (((1,), (1,)), ((), ()))
TN_DIMS = (((0,), (0,)), ((), ()))


def _dot(a, b):
    return jnp.dot(a, b, preferred_element_type=F32)


def _dot_nt(a, b):
    return lax.dot_general(a, b, NT_DIMS, preferred_element_type=F32)


def _dot_tn(a, b):
    return lax.dot_general(a, b, TN_DIMS, preferred_element_type=F32)


def _sigmoid(x):
    return 0.5 * jnp.tanh(0.5 * x) + 0.5


def _silu(x):
    h = 0.5 * x
    return h + h * jnp.tanh(h)


def _softplus(x):
    return jnp.maximum(x, 0.0) + jnp.log1p(jnp.exp(-jnp.abs(x)))


def _log_sigmoid(x):
    return jnp.minimum(x, 0.0) - jnp.log1p(jnp.exp(-jnp.abs(x)))


def _cumsum_lanes(x, seg):
    pos = lax.broadcasted_iota(jnp.int32, x.shape, 1) % seg
    s = 1
    while s < seg:
        x = x + jnp.where(pos >= s, pltpu.roll(x, s, 1), 0.0)
        s *= 2
    return x


def _rows_to_cols(r):
    t = r.shape[1]
    if r.shape[0] < LANES:
        r = jnp.concatenate([r, jnp.zeros((LANES - r.shape[0], t), F32)], axis=0)
    return r.T


def _head_rms(y):
    return y * lax.rsqrt(jnp.mean(y * y, axis=-1, keepdims=True) + NORM_EPS)


def _norm_proj_kernel(x_ref, g_ref, w_ref, wgt_ref, o_ref, grow_ref, h_ref):
    @pl.when(pl.program_id(1) == 0)
    def _():
        x = x_ref[...]
        y = x * lax.rsqrt(jnp.mean(x * x, axis=-1, keepdims=True) + NORM_EPS)
        hb = (y * g_ref[...]).astype(BF16)
        h_ref[...] = hb
        grow_ref[...] = _dot_nt(wgt_ref[...], hb)

    o_ref[...] = _dot(h_ref[...], w_ref[...]).astype(o_ref.dtype)


def _norm_proj(x2d, gain, w, wgt, *, tm, tn):
    m, d = x2d.shape
    n = w.shape[1]
    assert m % tm == 0 and n % tn == 0
    return pl.pallas_call(
        _norm_proj_kernel,
        grid=(m // tm, n // tn),
        in_specs=[
            pl.BlockSpec((tm, d), lambda i, j: (i, 0)),
            pl.BlockSpec((1, d), lambda i, j: (0, 0)),
            pl.BlockSpec((d, tn), lambda i, j: (0, j)),
            pl.BlockSpec((GATE_ROWS, d), lambda i, j: (0, 0)),
        ],
        out_specs=[
            pl.BlockSpec((tm, tn), lambda i, j: (i, j)),
            pl.BlockSpec((GATE_ROWS, tm), lambda i, j: (0, i)),
        ],
        out_shape=[
            jax.ShapeDtypeStruct((m, n), BF16),
            jax.ShapeDtypeStruct((GATE_ROWS, m), F32),
        ],
        scratch_shapes=[pltpu.VMEM((tm, d), BF16)],
        compiler_params=pltpu.CompilerParams(
            dimension_semantics=("arbitrary", "arbitrary"), vmem_limit_bytes=VMEM_LIMIT),
        name="norm_proj",
    )(x2d, gain.reshape(1, d), w, wgt)


def _norm_proj_gdn_kernel(x_ref, g_ref, w_ref, wgt_ref, cw_ref, o_ref, grow_ref, h_ref, carry_ref, *stage_refs,
                          seq_tiles):
    i = pl.program_id(0)
    j = pl.program_id(1)
    tm, tn = o_ref.shape
    pad = SUBLANES
    n_q = GDN_QK_W // tn
    n_qk = 2 * GDN_QK_W // tn
    n_conv = (2 * GDN_QK_W + GDN_V_W) // tn

    @pl.when(j == 0)
    def _():
        x = x_ref[...]
        y = x * lax.rsqrt(jnp.mean(x * x, axis=-1, keepdims=True) + NORM_EPS)
        hb = (y * g_ref[...]).astype(BF16)
        h_ref[...] = hb
        grow_ref[...] = _dot_nt(wgt_ref[...], hb)

    pieces = [slice(c, c + PROJ_PIECE) for c in range(0, tn, PROJ_PIECE)]

    def conv_rows(t, cs):
        y = cw_ref[CONV_WIDTH - 1:CONV_WIDTH, cs] * t
        for s in range(1, CONV_WIDTH):
            y = y + cw_ref[CONV_WIDTH - 1 - s:CONV_WIDTH - s, cs] * pltpu.roll(t, s, 0)
        return y

    def stage_products():
        for cs, st in zip(pieces, stage_refs):
            st[...] = _dot(h_ref[...], w_ref[:, cs])

    def conv_silu(cs, st):
        r = st[...]
        prev = jnp.where(i % seq_tiles == 0, 0.0, carry_ref[j, :, cs])
        carry_ref[j, :, cs] = r[tm - pad:tm, :]
        top = conv_rows(jnp.concatenate([prev, r[0:pad, :]], axis=0), cs)[pad:2 * pad, :]
        return _silu(jnp.concatenate([top, conv_rows(r, cs)[pad:, :]], axis=0))

    @pl.when(j < n_qk)
    def _():
        sc = jnp.where(j < n_q, GDN_DK ** -0.5, 1.0)
        stage_products()
        for cs, st in zip(pieces, stage_refs):
            y = conv_silu(cs, st)
            for hh in range(PROJ_PIECE // GDN_DK):
                yh = y[:, hh * GDN_DK:(hh + 1) * GDN_DK]
                inv = lax.rsqrt(jnp.sum(yh * yh, axis=-1, keepdims=True) + NORM_EPS) * sc
                o_ref[:, cs.start + hh * GDN_DK:cs.start + (hh + 1) * GDN_DK] = (yh * inv).astype(o_ref.dtype)

    @pl.when((j >= n_qk) & (j < n_conv))
    def _():
        stage_products()
        for cs, st in zip(pieces, stage_refs):
            o_ref[:, cs] = conv_silu(cs, st).astype(o_ref.dtype)

    @pl.when(j >= n_conv)
    def _():
        for cs in pieces:
            o_ref[:, cs] = _silu(_dot(h_ref[...], w_ref[:, cs])).astype(o_ref.dtype)


def _norm_proj_gdn(x2d, gain, w, wgt, conv_w, *, tm, tn, seq):
    m, d = x2d.shape
    n = w.shape[1]
    conv_c = conv_w.shape[1]
    assert m % tm == 0 and n % tn == 0 and seq % tm == 0
    assert GDN_QK_W % tn == 0 and conv_c % tn == 0 and tn % GDN_DK == 0
    n_conv = conv_c // tn
    return pl.pallas_call(
        functools.partial(_norm_proj_gdn_kernel, seq_tiles=seq // tm),
        grid=(m // tm, n // tn),
        in_specs=[
            pl.BlockSpec((tm, d), lambda i, j: (i, 0)),
            pl.BlockSpec((1, d), lambda i, j: (0, 0)),
            pl.BlockSpec((d, tn), lambda i, j: (0, j)),
            pl.BlockSpec((GATE_ROWS, d), lambda i, j: (0, 0)),
            pl.BlockSpec((CONV_WIDTH, tn), lambda i, j: (0, jnp.minimum(j, n_conv - 1))),
        ],
        out_specs=[
            pl.BlockSpec((tm, tn), lambda i, j: (i, j)),
            pl.BlockSpec((GATE_ROWS, tm), lambda i, j: (0, i)),
        ],
        out_shape=[
            jax.ShapeDtypeStruct((m, n), BF16),
            jax.ShapeDtypeStruct((GATE_ROWS, m), F32),
        ],
        scratch_shapes=[
            pltpu.VMEM((tm, d), BF16),
            pltpu.VMEM((n_conv, SUBLANES, tn), F32),
        ] + [pltpu.VMEM((tm, PROJ_PIECE), F32)] * (tn // PROJ_PIECE) + [
        ],
        compiler_params=pltpu.CompilerParams(
            dimension_semantics=("arbitrary", "arbitrary"), vmem_limit_bytes=VMEM_LIMIT),
        name="norm_proj_gdn",
    )(x2d, gain.reshape(1, d), w, wgt, conv_w)


def _out_proj_kernel(x_ref, *refs):
    o_ref = refs[-1]
    n_in = (len(refs) - 1) // 2
    acc = x_ref[...]
    for i in range(n_in):
        acc = acc + _dot(refs[i][...], refs[n_in + i][...])
    o_ref[...] = acc


def _out_proj(x2d, acts, w, *, tm):
    m, d = x2d.shape
    assert m % tm == 0 and sum(a.shape[1] for a in acts) == w.shape[0]
    offs = np.cumsum([0] + [a.shape[1] for a in acts])
    w_specs = []
    for a, off in zip(acts, offs):
        ka = a.shape[1]
        assert off % ka == 0
        w_specs.append(pl.BlockSpec((ka, d), functools.partial(lambda i, blk: (blk, 0), blk=int(off // ka)),
                                    pipeline_mode=pl.Buffered(1)))
    return pl.pallas_call(
        _out_proj_kernel,
        grid=(m // tm,),
        in_specs=(
            [pl.BlockSpec((tm, d), lambda i: (i, 0))]
            + [pl.BlockSpec((tm, a.shape[1]), lambda i: (i, 0)) for a in acts]
            + w_specs
        ),
        out_specs=pl.BlockSpec((tm, d), lambda i: (i, 0)),
        out_shape=jax.ShapeDtypeStruct((m, d), F32),
        compiler_params=pltpu.CompilerParams(
            dimension_semantics=("arbitrary",), vmem_limit_bytes=VMEM_LIMIT),
        name="out_proj",
    )(x2d, *acts, *([w] * len(acts)))


def _ffn_kernel(x_ref, g_ref, wg_ref, wu_ref, wd_ref, fg_ref, o_ref, h_ref, *, final_norm):
    j = pl.program_id(1)

    @pl.when(j == 0)
    def _():
        x = x_ref[...]
        y = x * lax.rsqrt(jnp.mean(x * x, axis=-1, keepdims=True) + NORM_EPS)
        h_ref[...] = (y * g_ref[...]).astype(BF16)
        o_ref[...] = x

    hb = h_ref[...]
    a = _dot(hb, wg_ref[...])
    b = _dot(hb, wu_ref[...])
    o_ref[...] += _dot((_silu(a) * b).astype(BF16), wd_ref[...])

    if final_norm:
        @pl.when(j == pl.num_programs(1) - 1)
        def _():
            x = o_ref[...]
            y = x * lax.rsqrt(jnp.mean(x * x, axis=-1, keepdims=True) + NORM_EPS)
            o_ref[...] = y * fg_ref[...]


def _ffn(x2d, gain, wg, wu, wd, final_gain, *, tm, tf, final_norm):
    m, d = x2d.shape
    f = wg.shape[1]
    assert m % tm == 0 and f % tf == 0
    return pl.pallas_call(
        functools.partial(_ffn_kernel, final_norm=final_norm),
        grid=(m // tm, f // tf),
        in_specs=[
            pl.BlockSpec((tm, d), lambda i, j: (i, 0)),
            pl.BlockSpec((1, d), lambda i, j: (0, 0)),
            pl.BlockSpec((d, tf), lambda i, j: (0, j)),
            pl.BlockSpec((d, tf), lambda i, j: (0, j)),
            pl.BlockSpec((tf, d), lambda i, j: (j, 0)),
            pl.BlockSpec((1, d), lambda i, j: (0, 0)),
        ],
        out_specs=pl.BlockSpec((tm, d), lambda i, j: (i, 0)),
        out_shape=jax.ShapeDtypeStruct((m, d), F32),
        scratch_shapes=[pltpu.VMEM((tm, d), BF16)],
        compiler_params=pltpu.CompilerParams(
            dimension_semantics=("arbitrary", "arbitrary"), vmem_limit_bytes=VMEM_LIMIT),
        name="ffn_final" if final_norm else "ffn",
    )(x2d, gain.reshape(1, d), wg, wu, wd, final_gain.reshape(1, d))


def _mlstm_kernel(q_ref, k_ref, v_ref, og_ref, grow_ref, bias_ref, norm_ref, o_ref,
                  c_ref, n_ref, m_ref):
    L = q_ref.shape[0]
    nh = MLSTM_HEADS

    @pl.when(pl.program_id(1) == 0)
    def _():
        c_ref[...] = jnp.zeros_like(c_ref)
        n_ref[...] = jnp.zeros_like(n_ref)
        m_ref[...] = jnp.full_like(m_ref, -jnp.inf)

    g = grow_ref[...] + bias_ref[...]
    g = GATE_SOFTCAP * jnp.tanh(g / GATE_SOFTCAP)
    row = lax.broadcasted_iota(jnp.int32, g.shape, 0)
    bcum = _cumsum_lanes(_log_sigmoid(g), L)
    rows = jnp.where(row < nh, g, bcum)
    cols = _rows_to_cols(rows)

    li = lax.broadcasted_iota(jnp.int32, (L, L), 0)
    si = lax.broadcasted_iota(jnp.int32, (L, L), 1)
    causal = si <= li
    scale = MLSTM_DQK ** -0.5

    st = []
    for h in range(nh):
        li_r = rows[h:h + 1, :]
        bc_r = rows[nh + h:nh + h + 1, :]
        li_c = cols[:, h:h + 1]
        bc_c = cols[:, nh + h:nh + h + 1]
        g_tot = bc_c[L - 1:L, :]
        m_prev = m_ref[h][:, 0:1]
        log_d = jnp.where(causal, bc_c + (li_r - bc_r), -jnp.inf)
        m_inter = bc_c + m_prev
        m_t = jnp.maximum(m_inter, jnp.max(log_d, axis=-1, keepdims=True))
        q = q_ref[:, h * MLSTM_DQK:(h + 1) * MLSTM_DQK]
        k = k_ref[:, h * MLSTM_DQK:(h + 1) * MLSTM_DQK]
        st.append(dict(
            q=q, k=k, v=v_ref[:, h * MLSTM_DV:(h + 1) * MLSTM_DV], c_mat=c_ref[h], n_vec=n_ref[h],
            m_t=m_t, inter=jnp.exp(m_inter - m_t) * scale,
            s=_dot_nt(q, k) * scale * jnp.exp(log_d - m_t),
            log_w=g_tot + (li_c - bc_c), g_tot=g_tot, m_prev=m_prev))

    for d in st:
        num = _dot(d["s"].astype(BF16), d["v"]) + d["inter"] * _dot(d["q"], d["c_mat"].astype(BF16))
        qn = jnp.sum(d["q"].astype(F32) * d["n_vec"], axis=-1, keepdims=True)
        den = jnp.sum(d["s"], axis=-1, keepdims=True) + d["inter"] * qn
        d["hout"] = num * (1.0 / jnp.maximum(jnp.abs(den), jnp.exp(-d["m_t"])))

    for h, d in enumerate(st):
        y = _head_rms(d["hout"]) * norm_ref[:, h * MLSTM_DV:(h + 1) * MLSTM_DV]
        og = og_ref[:, h * MLSTM_DV:(h + 1) * MLSTM_DV].astype(F32)
        o_ref[:, h * MLSTM_DV:(h + 1) * MLSTM_DV] = (_sigmoid(og) * y).astype(o_ref.dtype)

    for h, d in enumerate(st):
        m_w = jnp.max(d["log_w"], axis=0, keepdims=True)
        m_new = jnp.maximum(d["g_tot"] + d["m_prev"], m_w)
        decay = jnp.exp(d["g_tot"] + d["m_prev"] - m_new)
        kw = d["k"].astype(F32) * jnp.exp(d["log_w"] - m_new)
        c_ref[h] = decay * d["c_mat"] + _dot_tn(kw.astype(BF16), d["v"])
        n_ref[h] = decay * d["n_vec"] + jnp.sum(kw, axis=0, keepdims=True)
        m_ref[h] = jnp.broadcast_to(m_new, (1, LANES))


def _mlstm(proj, grow, gate_bias, norm_w, *, batch, seq):
    m = proj.shape[0]
    L = L0_CHUNK
    nc = seq // L
    tok = lambda b, c: b * nc + c
    return pl.pallas_call(
        _mlstm_kernel,
        grid=(batch, nc),
        in_specs=[
            pl.BlockSpec((L, MLSTM_QK_W), lambda b, c: (tok(b, c), 0)),
            pl.BlockSpec((L, MLSTM_QK_W), lambda b, c: (tok(b, c), 1)),
            pl.BlockSpec((L, MLSTM_V_W), lambda b, c: (tok(b, c), 1)),
            pl.BlockSpec((L, MLSTM_V_W), lambda b, c: (tok(b, c), 2)),
            pl.BlockSpec((SUBLANES, L), lambda b, c: (0, tok(b, c))),
            pl.BlockSpec((SUBLANES, 1), lambda b, c: (0, 0)),
            pl.BlockSpec((1, MLSTM_V_W), lambda b, c: (0, 0)),
        ],
        out_specs=pl.BlockSpec((L, MLSTM_V_W), lambda b, c: (tok(b, c), 0)),
        out_shape=jax.ShapeDtypeStruct((m, MLSTM_V_W), BF16),
        scratch_shapes=[
            pltpu.VMEM((MLSTM_HEADS, MLSTM_DQK, MLSTM_DV), F32),
            pltpu.VMEM((MLSTM_HEADS, 1, MLSTM_DQK), F32),
            pltpu.VMEM((MLSTM_HEADS, 1, LANES), F32),
        ],
        compiler_params=pltpu.CompilerParams(
            dimension_semantics=("arbitrary", "arbitrary"), vmem_limit_bytes=VMEM_LIMIT),
        name="mlstm",
    )(proj, proj, proj, proj, grow, gate_bias.reshape(2 * MLSTM_HEADS, 1), norm_w.reshape(1, MLSTM_V_W))


def _ret_log_gamma(h):
    return math.log(1.0 - 2.0 ** (-5.0 - h))


def _retention_kernel(q_ref, k_ref, v_ref, g_ref, pos_ref, invf_ref, o_ref, r_ref, d_ref):
    L = q_ref.shape[0]
    half = RET_DK // 2
    scale = RET_DK ** -0.5

    @pl.when((pl.program_id(0) == 0) & (pl.program_id(1) == 0))
    def _():
        li = lax.broadcasted_iota(jnp.int32, (L, L), 0)
        si = lax.broadcasted_iota(jnp.int32, (L, L), 1)
        rel = (li - si).astype(F32)
        for h in range(RET_HEADS):
            d_ref[h] = jnp.where(si <= li, jnp.exp(rel * _ret_log_gamma(h)), 0.0) * scale

    @pl.when(pl.program_id(1) == 0)
    def _():
        r_ref[...] = jnp.zeros_like(r_ref)

    ang_t = invf_ref[...] * pos_ref[...].astype(F32)
    cos = jnp.cos(ang_t).T
    sin = jnp.sin(ang_t).T
    p_c = lax.broadcasted_iota(jnp.int32, (L, 1), 0).astype(F32)

    def rot(t):
        t1 = t[:, :half].astype(F32)
        t2 = t[:, half:].astype(F32)
        return jnp.concatenate([t1 * cos - t2 * sin, t1 * sin + t2 * cos], axis=1)

    for h in range(RET_HEADS):
        lg = _ret_log_gamma(h)
        sl = slice(h * RET_DK, (h + 1) * RET_DK)
        qr = rot(q_ref[:, sl])
        kr = rot(k_ref[:, sl])
        v = v_ref[:, h * RET_DV:(h + 1) * RET_DV]
        qb = qr.astype(BF16)
        r_mat = r_ref[h]
        s = _dot_nt(qb, kr.astype(BF16)) * d_ref[h]
        intra = _dot(s.astype(BF16), v)
        inter = _dot(qb, r_mat.astype(BF16)) * (jnp.exp((p_c + 1.0) * lg) * scale)
        kw = kr * jnp.exp((L - 1.0 - p_c) * lg)
        r_ref[h] = math.exp(L * lg) * r_mat + _dot_tn(kw.astype(BF16), v)
        y = _head_rms(intra + inter) * _silu(g_ref[:, h * RET_DV:(h + 1) * RET_DV].astype(F32))
        o_ref[:, h * RET_DV:(h + 1) * RET_DV] = y.astype(o_ref.dtype)


def _retention(proj, positions, *, batch, seq):
    m = proj.shape[0]
    L = L0_CHUNK
    nc = seq // L
    tok = lambda b, c: b * nc + c
    half = RET_DK // 2
    inv_freq = ROPE_BASE ** (-jnp.arange(0, RET_DK, 2, dtype=F32) / RET_DK)
    base = (2 * MLSTM_QK_W + 2 * MLSTM_V_W) // RET_QK_W
    return pl.pallas_call(
        _retention_kernel,
        grid=(batch, nc),
        in_specs=[
            pl.BlockSpec((L, RET_QK_W), lambda b, c: (tok(b, c), base)),
            pl.BlockSpec((L, RET_QK_W), lambda b, c: (tok(b, c), base + 1)),
            pl.BlockSpec((L, RET_V_W), lambda b, c: (tok(b, c), base + 2)),
            pl.BlockSpec((L, RET_V_W), lambda b, c: (tok(b, c), base + 3)),
            pl.BlockSpec((None, 1, L), lambda b, c: (tok(b, c), 0, 0)),
            pl.BlockSpec((half, 1), lambda b, c: (0, 0)),
        ],
        out_specs=pl.BlockSpec((L, RET_V_W), lambda b, c: (tok(b, c), 0)),
        out_shape=jax.ShapeDtypeStruct((m, RET_V_W), BF16),
        scratch_shapes=[
            pltpu.VMEM((RET_HEADS, RET_DK, RET_DV), F32),
            pltpu.VMEM((RET_HEADS, L, L), F32),
        ],
        compiler_params=pltpu.CompilerParams(
            dimension_semantics=("arbitrary", "arbitrary"), vmem_limit_bytes=VMEM_LIMIT),
        name="retention",
    )(proj, proj, proj, proj, positions.reshape(batch * nc, 1, L), inv_freq.reshape(half, 1))


def _gdn_kernel(q_ref, k_ref, v_ref, z_ref, grow_ref, alog_ref, dtb_ref, norm_ref, o_ref, s_ref):
    T = q_ref.shape[0]
    C = GDN_CHUNK
    R = GDN_REP * C
    nch = T // C
    assert GDN_REP == 2 and R == LANES

    @pl.when(pl.program_id(2) == 0)
    def _():
        s_ref[...] = jnp.zeros_like(s_ref)

    g = grow_ref[...]
    row = lax.broadcasted_iota(jnp.int32, g.shape, 0) % SUBLANES
    beta = _sigmoid(g)
    log_alpha = -jnp.exp(alog_ref[...]) * _softplus(g + dtb_ref[...])
    gcum = _cumsum_lanes(log_alpha, C)
    rows = jnp.where(row < GDN_REP, beta, gcum)
    cols = _rows_to_cols(rows)

    ri = lax.broadcasted_iota(jnp.int32, (R, R), 0)
    ci = lax.broadcasted_iota(jnp.int32, (R, R), 1)
    same = (ri // C) == (ci // C)
    causal = same & (ci <= ri)
    strict = same & (ci < ri)
    eye = jnp.where(ri == ci, 1.0, 0.0)
    top = lax.broadcasted_iota(jnp.int32, (R, GDN_DV), 0) < C

    def stack2(a, b):
        return jnp.concatenate([a, b], axis=0)

    pre = {}

    def prepare(c, h):
        rs = slice(c * C, (c + 1) * C)
        ks = slice(h * GDN_DK, (h + 1) * GDN_DK)
        q_c = q_ref[rs, ks]
        k_c = k_ref[rs, ks]
        k2b = stack2(k_c, k_c)
        q2b = stack2(q_c, q_c)
        k2 = k2b.astype(F32)
        q2 = q2b.astype(F32)
        base = h * SUBLANES
        beta2 = stack2(cols[rs, base:base + 1], cols[rs, base + 1:base + 2])
        gc2 = stack2(cols[rs, base + GDN_REP:base + GDN_REP + 1], cols[rs, base + GDN_REP + 1:base + GDN_REP + 2])
        gc2_r = jnp.concatenate([rows[base + GDN_REP:base + GDN_REP + 1, rs],
                                 rows[base + GDN_REP + 1:base + GDN_REP + 2, rs]], axis=1)
        gl0 = gc2[C - 1:C, :]
        gl1 = gc2[R - 1:R, :]
        bt_w = jnp.broadcast_to(beta2, (R, LANES))
        gc_w = jnp.broadcast_to(gc2, (R, LANES))
        gl_w = jnp.where(top, gl0, gl1)
        eg_w = jnp.exp(gc_w)
        decay = jnp.exp(jnp.where(causal, gc_w - gc2_r, -jnp.inf))
        kq = _dot_nt(jnp.concatenate([k2b, q2b], axis=0), k2b)
        kk = kq[:R]
        qk = kq[R:]
        v2 = stack2(v_ref[rs, (2 * h) * GDN_DV:(2 * h + 1) * GDN_DV],
                    v_ref[rs, (2 * h + 1) * GDN_DV:(2 * h + 2) * GDN_DV]).astype(F32)
        pre[(c, h)] = dict(
            a=jnp.where(strict, bt_w * kk * decay, 0.0),
            attn=(qk * decay).astype(BF16),
            rhs=jnp.concatenate([v2 * bt_w, k2 * (bt_w * eg_w)], axis=1).astype(BF16),
            qg=(q2 * eg_w).astype(BF16),
            kt=(k2 * jnp.exp(gl_w - gc_w)).astype(BF16),
            gt=jnp.concatenate([jnp.broadcast_to(jnp.exp(gl0), (1, GDN_DV)),
                                jnp.broadcast_to(jnp.exp(gl1), (1, GDN_DV))], axis=1),
        )

    def solve(pairs):
        p = {}
        t = {}
        for key in pairs:
            p0 = -pre[key]["a"]
            pb = p0.astype(BF16)
            t[key] = eye + p0
            p[key] = _dot(pb, pb)
        step = 4
        while step < C:
            for key in pairs:
                pb = p[key].astype(BF16)
                x = _dot(jnp.concatenate([t[key].astype(BF16), pb], axis=0), pb)
                t[key] = t[key] + x[:R]
                p[key] = x[R:]
            step *= 2
        for key in pairs:
            t[key] = t[key] + _dot(t[key].astype(BF16), p[key].astype(BF16))
        return {key: _dot(t[key].astype(BF16), pre[key]["rhs"]) for key in pairs}

    s2 = [s_ref[h] for h in range(GDN_GROUP)]

    def recur(c, uw):
        rs = slice(c * C, (c + 1) * C)
        x1 = []
        for h in range(GDN_GROUP):
            w2b = uw[(c, h)][:, GDN_DV:].astype(BF16)
            lhs = jnp.concatenate([w2b, pre[(c, h)]["qg"]], axis=0)
            x1.append(_dot(lhs, s2[h].astype(BF16)))
        for h in range(GDN_GROUP):
            d = pre[(c, h)]
            ws = jnp.where(top, x1[h][:R, :GDN_DV], x1[h][:R, GDN_DV:])
            qs = jnp.where(top, x1[h][R:, :GDN_DV], x1[h][R:, GDN_DV:])
            vn = uw[(c, h)][:, :GDN_DV] - ws
            out = qs + _dot(d["attn"], vn.astype(BF16))
            vx = jnp.concatenate([jnp.where(top, vn, 0.0), jnp.where(top, 0.0, vn)], axis=1).astype(BF16)
            s2[h] = s2[h] * d["gt"] + _dot_tn(d["kt"], vx)
            z2 = stack2(z_ref[rs, (2 * h) * GDN_DV:(2 * h + 1) * GDN_DV],
                        z_ref[rs, (2 * h + 1) * GDN_DV:(2 * h + 2) * GDN_DV]).astype(F32)
            y = (_head_rms(out) * norm_ref[...] * z2).astype(o_ref.dtype)
            o_ref[rs, (2 * h) * GDN_DV:(2 * h + 1) * GDN_DV] = y[:C]
            o_ref[rs, (2 * h + 1) * GDN_DV:(2 * h + 2) * GDN_DV] = y[C:]

    for c0 in range(0, nch, GDN_PHASE_CHUNKS):
        group = [(c, h) for c in range(c0, c0 + GDN_PHASE_CHUNKS) for h in range(GDN_GROUP)]
        for key in group:
            prepare(*key)
        uw = solve(group)
        for c in range(c0, c0 + GDN_PHASE_CHUNKS):
            recur(c, uw)
    for h in range(GDN_GROUP):
        s_ref[h] = s2[h]


def _gdn(proj, grow, a_log, dt_bias, norm_w, *, batch, seq):
    m = proj.shape[0]
    T = GDN_BLOCK
    G = GDN_GROUP
    nt = seq // T
    ng = GDN_QK_HEADS // G
    qw = G * GDN_DK
    vw = G * GDN_REP * GDN_DV
    tok = lambda b, h, t: b * nt + t
    kbase = GDN_QK_W // qw
    vbase = 2 * GDN_QK_W // vw
    zbase = (2 * GDN_QK_W + GDN_V_W) // vw

    def per_head_rows(p):
        t = jnp.zeros((GDN_QK_HEADS, SUBLANES), F32)
        t = t.at[:, GDN_REP:2 * GDN_REP].set(p.astype(F32).reshape(GDN_QK_HEADS, GDN_REP))
        return t.reshape(GDN_QK_HEADS * SUBLANES, 1)

    return pl.pallas_call(
        _gdn_kernel,
        grid=(batch, ng, nt),
        in_specs=[
            pl.BlockSpec((T, qw), lambda b, h, t: (tok(b, h, t), h)),
            pl.BlockSpec((T, qw), lambda b, h, t: (tok(b, h, t), kbase + h)),
            pl.BlockSpec((T, vw), lambda b, h, t: (tok(b, h, t), vbase + h)),
            pl.BlockSpec((T, vw), lambda b, h, t: (tok(b, h, t), zbase + h)),
            pl.BlockSpec((G * SUBLANES, T), lambda b, h, t: (h, tok(b, h, t))),
            pl.BlockSpec((G * SUBLANES, 1), lambda b, h, t: (h, 0)),
            pl.BlockSpec((G * SUBLANES, 1), lambda b, h, t: (h, 0)),
            pl.BlockSpec((1, GDN_DV), lambda b, h, t: (0, 0)),
        ],
        out_specs=pl.BlockSpec((T, vw), lambda b, h, t: (tok(b, h, t), h)),
        out_shape=jax.ShapeDtypeStruct((m, GDN_V_W), BF16),
        scratch_shapes=[pltpu.VMEM((G, GDN_DK, GDN_REP * GDN_DV), F32)],
        compiler_params=pltpu.CompilerParams(
            dimension_semantics=("arbitrary", "arbitrary", "arbitrary"), vmem_limit_bytes=VMEM_LIMIT),
        name="gdn",
    )(proj, proj, proj, proj, grow, per_head_rows(a_log), per_head_rows(dt_bias), norm_w.reshape(1, GDN_DV))


def _l0_weights(w_in):
    d = w_in.shape[0]
    o = np.cumsum([0, MLSTM_QK_W, MLSTM_QK_W, MLSTM_V_W, MLSTM_V_W, 2 * MLSTM_HEADS,
                   RET_QK_W, RET_QK_W, RET_V_W, RET_V_W])
    def deinterleave(w):
        return w.reshape(d, RET_HEADS, RET_DK // 2, 2).transpose(0, 1, 3, 2).reshape(d, RET_QK_W)

    w_main = jnp.concatenate([
        w_in[:, o[0]:o[4]].astype(BF16),
        deinterleave(w_in[:, o[5]:o[6]].astype(BF16)),
        deinterleave(w_in[:, o[6]:o[7]].astype(BF16)),
        w_in[:, o[7]:o[9]].astype(BF16),
    ], axis=1)
    gates = w_in[:, o[4]:o[5]].T
    wgt = jnp.zeros((GATE_ROWS, d), F32).at[:2 * MLSTM_HEADS].set(gates).astype(BF16)
    return w_main, wgt


def _l1_weights(w_in):
    d = w_in.shape[0]
    main_w = 2 * GDN_QK_W + 2 * GDN_V_W
    w_main = w_in[:, :main_w].astype(BF16)
    b_pre = w_in[:, main_w:main_w + GDN_V_HEADS].T.reshape(GDN_QK_HEADS, GDN_REP, d)
    a_pre = w_in[:, main_w + GDN_V_HEADS:main_w + 2 * GDN_V_HEADS].T.reshape(GDN_QK_HEADS, GDN_REP, d)
    wgt = jnp.zeros((GDN_QK_HEADS, SUBLANES, d), F32)
    wgt = wgt.at[:, 0:GDN_REP].set(b_pre).at[:, GDN_REP:2 * GDN_REP].set(a_pre)
    return w_main, wgt.reshape(GATE_ROWS, d).astype(BF16)


def kernel(x, positions, l0_norm_mix, l0_w_in, l0_gate_bias, l0_mlstm_norm, l0_w_out, l0_norm_ffn, l0_w_gate,
           l0_w_up, l0_w_down, l1_norm_mix, l1_w_in, l1_conv_w, l1_a_log, l1_dt_bias, l1_gdn_norm, l1_w_out,
           l1_norm_ffn, l1_w_gate, l1_w_up, l1_w_down, final_norm):
    batch, seq, d = x.shape
    m = batch * seq
    x0 = x.reshape(m, d)

    w0, wgt0 = _l0_weights(l0_w_in)
    proj0, grow0 = _norm_proj(x0, l0_norm_mix, w0, wgt0, tm=1024, tn=1024)
    h_m = _mlstm(proj0, grow0, l0_gate_bias, l0_mlstm_norm, batch=batch, seq=seq)
    y_r = _retention(proj0, positions, batch=batch, seq=seq)
    x1 = _out_proj(x0, [h_m, y_r], l0_w_out.astype(BF16), tm=512)
    x1 = _ffn(x1, l0_norm_ffn, l0_w_gate.astype(BF16), l0_w_up.astype(BF16), l0_w_down.astype(BF16),
              final_norm, tm=512, tf=512, final_norm=False)

    w1, wgt1 = _l1_weights(l1_w_in)
    proj1, grow1 = _norm_proj_gdn(x1, l1_norm_mix, w1, wgt1, l1_conv_w, tm=1024, tn=1024, seq=seq)
    o = _gdn(proj1, grow1, l1_a_log, l1_dt_bias, l1_gdn_norm, batch=batch, seq=seq)
    x2 = _out_proj(x1, [o], l1_w_out.astype(BF16), tm=512)
    out = _ffn(x2, l1_norm_ffn, l1_w_gate.astype(BF16), l1_w_up.astype(BF16), l1_w_down.astype(BF16),
               final_norm, tm=512, tf=512, final_norm=True)
    return out.reshape(batch, seq, d)
```

```python
import functools
import math

import jax
import jax.numpy as jnp
import numpy as np
from jax import lax
from jax.experimental import pallas as pl
from jax.experimental.pallas import tpu as pltpu

F32 = jnp.float32
BF16 = jnp.bfloat16

D_MODEL = 2048
NORM_EPS = 1e-6
MLSTM_HEADS = 4
MLSTM_DQK = 128
MLSTM_DV = 256
GATE_SOFTCAP = 15.0
RET_HEADS = 4
RET_DK = 256
RET_DV = 256
ROPE_BASE = 10000.0
GDN_QK_HEADS = 16
GDN_V_HEADS = 32
GDN_DK = 128
GDN_DV = 128
GDN_REP = GDN_V_HEADS // GDN_QK_HEADS
CONV_WIDTH = 4
D_FF = -(-8 * D_MODEL // (3 * 256)) * 256

MLSTM_QK_W = MLSTM_HEADS * MLSTM_DQK
MLSTM_V_W = MLSTM_HEADS * MLSTM_DV
RET_QK_W = RET_HEADS * RET_DK
RET_V_W = RET_HEADS * RET_DV
GDN_QK_W = GDN_QK_HEADS * GDN_DK
GDN_V_W = GDN_V_HEADS * GDN_DV

LANES = 128
SUBLANES = 8
MXU_WIDTH = 256
PROJ_PIECE = MXU_WIDTH
GATE_ROWS = 128
VMEM_LIMIT = 56 * 1024 * 1024

L0_CHUNK = 256
GDN_CHUNK = 64
GDN_BLOCK = 512
GDN_GROUP = 4
GDN_PHASE_CHUNKS = 2

NT_DIMS = (((1,), (1,)), ((), ()))
TN_DIMS = (((0,), (0,)), ((), ()))


def _dot(a, b):
    return jnp.dot(a, b, preferred_element_type=F32)


def _dot_nt(a, b):
    return lax.dot_general(a, b, NT_DIMS, preferred_element_type=F32)


def _dot_tn(a, b):
    return lax.dot_general(a, b, TN_DIMS, preferred_element_type=F32)


def _sigmoid(x):
    return 0.5 * jnp.tanh(0.5 * x) + 0.5


def _silu(x):
    h = 0.5 * x
    return h + h * jnp.tanh(h)


def _softplus(x):
    return jnp.maximum(x, 0.0) + jnp.log1p(jnp.exp(-jnp.abs(x)))


def _log_sigmoid(x):
    return jnp.minimum(x, 0.0) - jnp.log1p(jnp.exp(-jnp.abs(x)))


def _cumsum_lanes(x, seg):
    pos = lax.broadcasted_iota(jnp.int32, x.shape, 1) % seg
    s = 1
    while s < seg:
        x = x + jnp.where(pos >= s, pltpu.roll(x, s, 1), 0.0)
        s *= 2
    return x


def _rows_to_cols(r):
    t = r.shape[1]
    if r.shape[0] < LANES:
        r = jnp.concatenate([r, jnp.zeros((LANES - r.shape[0], t), F32)], axis=0)
    return r.T


def _head_rms(y):
    return y * lax.rsqrt(jnp.mean(y * y, axis=-1, keepdims=True) + NORM_EPS)


def _norm_proj_kernel(x_ref, g_ref, w_ref, wgt_ref, bias_ref, o_ref, grow_ref, h_ref):
    @pl.when(pl.program_id(1) == 0)
    def _():
        x = x_ref[...]
        y = x * lax.rsqrt(jnp.mean(x * x, axis=-1, keepdims=True) + NORM_EPS)
        hb = (y * g_ref[...]).astype(BF16)
        h_ref[...] = hb
        g = _dot_nt(wgt_ref[...], hb) + bias_ref[...]
        g = GATE_SOFTCAP * jnp.tanh(g / GATE_SOFTCAP)
        row = lax.broadcasted_iota(jnp.int32, g.shape, 0)
        grow_ref[...] = jnp.where(row < MLSTM_HEADS, g, _cumsum_lanes(_log_sigmoid(g), L0_CHUNK))

    o_ref[...] = _dot(h_ref[...], w_ref[...]).astype(o_ref.dtype)


def _norm_proj(x2d, gain, w, wgt, gate_bias, *, tm, tn):
    m, d = x2d.shape
    n = w.shape[1]
    rows = 2 * MLSTM_HEADS
    assert m % tm == 0 and n % tn == 0 and tm % L0_CHUNK == 0 and wgt.shape == (rows, d)
    return pl.pallas_call(
        _norm_proj_kernel,
        grid=(m // tm, n // tn),
        in_specs=[
            pl.BlockSpec((tm, d), lambda i, j: (i, 0)),
            pl.BlockSpec((1, d), lambda i, j: (0, 0)),
            pl.BlockSpec((d, tn), lambda i, j: (0, j)),
            pl.BlockSpec((rows, d), lambda i, j: (0, 0)),
            pl.BlockSpec((rows, 1), lambda i, j: (0, 0)),
        ],
        out_specs=[
            pl.BlockSpec((tm, tn), lambda i, j: (i, j)),
            pl.BlockSpec((rows, tm), lambda i, j: (0, i)),
        ],
        out_shape=[
            jax.ShapeDtypeStruct((m, n), BF16),
            jax.ShapeDtypeStruct((rows, m), F32),
        ],
        scratch_shapes=[pltpu.VMEM((tm, d), BF16)],
        compiler_params=pltpu.CompilerParams(
            dimension_semantics=("arbitrary", "arbitrary"), vmem_limit_bytes=VMEM_LIMIT),
        name="norm_proj",
    )(x2d, gain.reshape(1, d), w, wgt, gate_bias.astype(F32).reshape(rows, 1))


def _norm_proj_gdn_kernel(x_ref, g_ref, w_ref, wgt_ref, cw_ref, alog_ref, dtb_ref, o_ref, grow_ref, h_ref, carry_ref,
                          *stage_refs, seq_tiles):
    i = pl.program_id(0)
    j = pl.program_id(1)
    tm, tn = o_ref.shape
    pad = SUBLANES
    n_q = GDN_QK_W // tn
    n_qk = 2 * GDN_QK_W // tn
    n_conv = (2 * GDN_QK_W + GDN_V_W) // tn

    @pl.when(j == 0)
    def _():
        x = x_ref[...]
        y = x * lax.rsqrt(jnp.mean(x * x, axis=-1, keepdims=True) + NORM_EPS)
        hb = (y * g_ref[...]).astype(BF16)
        h_ref[...] = hb
        g = _dot_nt(wgt_ref[...], hb)
        row = lax.broadcasted_iota(jnp.int32, g.shape, 0) % SUBLANES
        log_alpha = -jnp.exp(alog_ref[...]) * _softplus(g + dtb_ref[...])
        grow_ref[...] = jnp.where(row < GDN_REP, _sigmoid(g), _cumsum_lanes(log_alpha, GDN_CHUNK))

    pieces = [slice(c, c + PROJ_PIECE) for c in range(0, tn, PROJ_PIECE)]

    def conv_rows(t, cs):
        y = cw_ref[CONV_WIDTH - 1:CONV_WIDTH, cs] * t
        for s in range(1, CONV_WIDTH):
            y = y + cw_ref[CONV_WIDTH - 1 - s:CONV_WIDTH - s, cs] * pltpu.roll(t, s, 0)
        return y

    def stage_products():
        for cs, st in zip(pieces, stage_refs):
            st[...] = _dot(h_ref[...], w_ref[:, cs])

    def conv_silu(cs, st):
        r = st[...]
        prev = jnp.where(i % seq_tiles == 0, 0.0, carry_ref[j, :, cs])
        carry_ref[j, :, cs] = r[tm - pad:tm, :]
        top = conv_rows(jnp.concatenate([prev, r[0:pad, :]], axis=0), cs)[pad:2 * pad, :]
        return _silu(jnp.concatenate([top, conv_rows(r, cs)[pad:, :]], axis=0))

    @pl.when(j < n_qk)
    def _():
        sc = jnp.where(j < n_q, GDN_DK ** -0.5, 1.0)
        stage_products()
        for cs, st in zip(pieces, stage_refs):
            y = conv_silu(cs, st)
            for hh in range(PROJ_PIECE // GDN_DK):
                yh = y[:, hh * GDN_DK:(hh + 1) * GDN_DK]
                inv = lax.rsqrt(jnp.sum(yh * yh, axis=-1, keepdims=True) + NORM_EPS) * sc
                o_ref[:, cs.start + hh * GDN_DK:cs.start + (hh + 1) * GDN_DK] = (yh * inv).astype(o_ref.dtype)

    @pl.when((j >= n_qk) & (j < n_conv))
    def _():
        stage_products()
        for cs, st in zip(pieces, stage_refs):
            o_ref[:, cs] = conv_silu(cs, st).astype(o_ref.dtype)

    @pl.when(j >= n_conv)
    def _():
        for cs in pieces:
            o_ref[:, cs] = _silu(_dot(h_ref[...], w_ref[:, cs])).astype(o_ref.dtype)


def _norm_proj_gdn(x2d, gain, w, wgt, conv_w, a_log, dt_bias, *, tm, tn, seq):
    m, d = x2d.shape
    n = 2 * GDN_QK_W + 2 * GDN_V_W
    conv_c = conv_w.shape[1]
    assert m % tm == 0 and n % tn == 0 and seq % tm == 0 and tm % GDN_CHUNK == 0
    assert GDN_QK_W % tn == 0 and conv_c % tn == 0 and tn % GDN_DK == 0
    n_conv = conv_c // tn

    def per_head_rows(p):
        t = jnp.zeros((GDN_QK_HEADS, SUBLANES), F32)
        t = t.at[:, GDN_REP:2 * GDN_REP].set(p.astype(F32).reshape(GDN_QK_HEADS, GDN_REP))
        return t.reshape(GATE_ROWS, 1)

    return pl.pallas_call(
        functools.partial(_norm_proj_gdn_kernel, seq_tiles=seq // tm),
        grid=(m // tm, n // tn),
        in_specs=[
            pl.BlockSpec((tm, d), lambda i, j: (i, 0)),
            pl.BlockSpec((1, d), lambda i, j: (0, 0)),
            pl.BlockSpec((d, tn), lambda i, j: (0, j)),
            pl.BlockSpec((GATE_ROWS, d), lambda i, j: (0, 0)),
            pl.BlockSpec((CONV_WIDTH, tn), lambda i, j: (0, jnp.minimum(j, n_conv - 1))),
            pl.BlockSpec((GATE_ROWS, 1), lambda i, j: (0, 0)),
            pl.BlockSpec((GATE_ROWS, 1), lambda i, j: (0, 0)),
        ],
        out_specs=[
            pl.BlockSpec((tm, tn), lambda i, j: (i, j)),
            pl.BlockSpec((GATE_ROWS, tm), lambda i, j: (0, i)),
        ],
        out_shape=[
            jax.ShapeDtypeStruct((m, n), BF16),
            jax.ShapeDtypeStruct((GATE_ROWS, m), F32),
        ],
        scratch_shapes=[
            pltpu.VMEM((tm, d), BF16),
            pltpu.VMEM((n_conv, SUBLANES, tn), F32),
        ] + [pltpu.VMEM((tm, PROJ_PIECE), F32)] * (tn // PROJ_PIECE),
        compiler_params=pltpu.CompilerParams(
            dimension_semantics=("arbitrary", "arbitrary"), vmem_limit_bytes=VMEM_LIMIT),
        name="norm_proj_gdn",
    )(x2d, gain.reshape(1, d), w, wgt, conv_w, per_head_rows(a_log), per_head_rows(dt_bias))


def _out_proj_kernel(x_ref, *refs):
    o_ref = refs[-1]
    n_in = (len(refs) - 1) // 2
    acc = x_ref[...]
    for i in range(n_in):
        acc = acc + _dot(refs[i][...], refs[n_in + i][...])
    o_ref[...] = acc


def _out_proj(x2d, acts, w, *, tm):
    m, d = x2d.shape
    assert m % tm == 0 and sum(a.shape[1] for a in acts) == w.shape[0]
    offs = np.cumsum([0] + [a.shape[1] for a in acts])
    w_specs = []
    for a, off in zip(acts, offs):
        ka = a.shape[1]
        assert off % ka == 0
        w_specs.append(pl.BlockSpec((ka, d), functools.partial(lambda i, blk: (blk, 0), blk=int(off // ka)),
                                    pipeline_mode=pl.Buffered(1)))
    return pl.pallas_call(
        _out_proj_kernel,
        grid=(m // tm,),
        in_specs=(
            [pl.BlockSpec((tm, d), lambda i: (i, 0))]
            + [pl.BlockSpec((tm, a.shape[1]), lambda i: (i, 0)) for a in acts]
            + w_specs
        ),
        out_specs=pl.BlockSpec((tm, d), lambda i: (i, 0)),
        out_shape=jax.ShapeDtypeStruct((m, d), F32),
        compiler_params=pltpu.CompilerParams(
            dimension_semantics=("arbitrary",), vmem_limit_bytes=VMEM_LIMIT),
        name="out_proj",
    )(x2d, *acts, *([w] * len(acts)))


def _ffn_kernel(x_ref, g_ref, wg_ref, wu_ref, wd_ref, fg_ref, o_ref, h_ref, *, final_norm):
    j = pl.program_id(1)

    @pl.when(j == 0)
    def _():
        x = x_ref[...]
        y = x * lax.rsqrt(jnp.mean(x * x, axis=-1, keepdims=True) + NORM_EPS)
        h_ref[...] = (y * g_ref[...]).astype(BF16)
        o_ref[...] = x

    hb = h_ref[...]
    a = _dot(hb, wg_ref[...])
    b = _dot(hb, wu_ref[...])
    o_ref[...] += _dot((_silu(a) * b).astype(BF16), wd_ref[...])

    if final_norm:
        @pl.when(j == pl.num_programs(1) - 1)
        def _():
            x = o_ref[...]
            y = x * lax.rsqrt(jnp.mean(x * x, axis=-1, keepdims=True) + NORM_EPS)
            o_ref[...] = y * fg_ref[...]


def _ffn(x2d, gain, wg, wu, wd, final_gain, *, tm, tf, final_norm):
    m, d = x2d.shape
    f = wg.shape[1]
    assert m % tm == 0 and f % tf == 0
    return pl.pallas_call(
        functools.partial(_ffn_kernel, final_norm=final_norm),
        grid=(m // tm, f // tf),
        in_specs=[
            pl.BlockSpec((tm, d), lambda i, j: (i, 0)),
            pl.BlockSpec((1, d), lambda i, j: (0, 0)),
            pl.BlockSpec((d, tf), lambda i, j: (0, j)),
            pl.BlockSpec((d, tf), lambda i, j: (0, j)),
            pl.BlockSpec((tf, d), lambda i, j: (j, 0)),
            pl.BlockSpec((1, d), lambda i, j: (0, 0)),
        ],
        out_specs=pl.BlockSpec((tm, d), lambda i, j: (i, 0)),
        out_shape=jax.ShapeDtypeStruct((m, d), F32),
        scratch_shapes=[pltpu.VMEM((tm, d), BF16)],
        compiler_params=pltpu.CompilerParams(
            dimension_semantics=("arbitrary", "arbitrary"), vmem_limit_bytes=VMEM_LIMIT),
        name="ffn_final" if final_norm else "ffn",
    )(x2d, gain.reshape(1, d), wg, wu, wd, final_gain.reshape(1, d))


def _mlstm_kernel(q_ref, k_ref, v_ref, og_ref, grow_ref, norm_ref, o_ref, c_ref, n_ref, m_ref):
    L = q_ref.shape[0]
    nh = MLSTM_HEADS

    @pl.when(pl.program_id(1) == 0)
    def _():
        c_ref[...] = jnp.zeros_like(c_ref)
        n_ref[...] = jnp.zeros_like(n_ref)
        m_ref[...] = jnp.full_like(m_ref, -jnp.inf)

    rows = grow_ref[...]
    cols = _rows_to_cols(rows)

    li = lax.broadcasted_iota(jnp.int32, (L, L), 0)
    si = lax.broadcasted_iota(jnp.int32, (L, L), 1)
    causal = si <= li
    scale = MLSTM_DQK ** -0.5

    st = []
    for h in range(nh):
        li_r = rows[h:h + 1, :]
        bc_r = rows[nh + h:nh + h + 1, :]
        li_c = cols[:, h:h + 1]
        bc_c = cols[:, nh + h:nh + h + 1]
        g_tot = bc_c[L - 1:L, :]
        m_prev = m_ref[h][:, 0:1]
        log_d = jnp.where(causal, bc_c + (li_r - bc_r), -jnp.inf)
        m_inter = bc_c + m_prev
        m_t = jnp.maximum(m_inter, jnp.max(log_d, axis=-1, keepdims=True))
        q = q_ref[:, h * MLSTM_DQK:(h + 1) * MLSTM_DQK]
        k = k_ref[:, h * MLSTM_DQK:(h + 1) * MLSTM_DQK]
        st.append(dict(
            q=q, k=k, v=v_ref[:, h * MLSTM_DV:(h + 1) * MLSTM_DV], c_mat=c_ref[h], n_vec=n_ref[h],
            m_t=m_t, inter=jnp.exp(m_inter - m_t) * scale,
            s=_dot_nt(q, k) * scale * jnp.exp(log_d - m_t),
            log_w=g_tot + (li_c - bc_c), g_tot=g_tot, m_prev=m_prev))

    for d in st:
        num = _dot(d["s"].astype(BF16), d["v"]) + d["inter"] * _dot(d["q"], d["c_mat"].astype(BF16))
        qn = jnp.sum(d["q"].astype(F32) * d["n_vec"], axis=-1, keepdims=True)
        den = jnp.sum(d["s"], axis=-1, keepdims=True) + d["inter"] * qn
        d["hout"] = num * (1.0 / jnp.maximum(jnp.abs(den), jnp.exp(-d["m_t"])))

    for h, d in enumerate(st):
        y = _head_rms(d["hout"]) * norm_ref[:, h * MLSTM_DV:(h + 1) * MLSTM_DV]
        og = og_ref[:, h * MLSTM_DV:(h + 1) * MLSTM_DV].astype(F32)
        o_ref[:, h * MLSTM_DV:(h + 1) * MLSTM_DV] = (_sigmoid(og) * y).astype(o_ref.dtype)

    for h, d in enumerate(st):
        m_w = jnp.max(d["log_w"], axis=0, keepdims=True)
        m_new = jnp.maximum(d["g_tot"] + d["m_prev"], m_w)
        decay = jnp.exp(d["g_tot"] + d["m_prev"] - m_new)
        kw = d["k"].astype(F32) * jnp.exp(d["log_w"] - m_new)
        c_ref[h] = decay * d["c_mat"] + _dot_tn(kw.astype(BF16), d["v"])
        n_ref[h] = decay * d["n_vec"] + jnp.sum(kw, axis=0, keepdims=True)
        m_ref[h] = jnp.broadcast_to(m_new, (1, LANES))


def _mlstm(proj, grow, norm_w, *, batch, seq):
    m = proj.shape[0]
    L = L0_CHUNK
    nc = seq // L
    tok = lambda b, c: b * nc + c
    return pl.pallas_call(
        _mlstm_kernel,
        grid=(batch, nc),
        in_specs=[
            pl.BlockSpec((L, MLSTM_QK_W), lambda b, c: (tok(b, c), 0)),
            pl.BlockSpec((L, MLSTM_QK_W), lambda b, c: (tok(b, c), 1)),
            pl.BlockSpec((L, MLSTM_V_W), lambda b, c: (tok(b, c), 1)),
            pl.BlockSpec((L, MLSTM_V_W), lambda b, c: (tok(b, c), 2)),
            pl.BlockSpec((SUBLANES, L), lambda b, c: (0, tok(b, c))),
            pl.BlockSpec((1, MLSTM_V_W), lambda b, c: (0, 0)),
        ],
        out_specs=pl.BlockSpec((L, MLSTM_V_W), lambda b, c: (tok(b, c), 0)),
        out_shape=jax.ShapeDtypeStruct((m, MLSTM_V_W), BF16),
        scratch_shapes=[
            pltpu.VMEM((MLSTM_HEADS, MLSTM_DQK, MLSTM_DV), F32),
            pltpu.VMEM((MLSTM_HEADS, 1, MLSTM_DQK), F32),
            pltpu.VMEM((MLSTM_HEADS, 1, LANES), F32),
        ],
        compiler_params=pltpu.CompilerParams(
            dimension_semantics=("arbitrary", "arbitrary"), vmem_limit_bytes=VMEM_LIMIT),
        name="mlstm",
    )(proj, proj, proj, proj, grow, norm_w.reshape(1, MLSTM_V_W))


def _ret_log_gamma(h):
    return math.log(1.0 - 2.0 ** (-5.0 - h))


def _retention_kernel(q_ref, k_ref, v_ref, g_ref, pos_ref, invf_ref, o_ref, r_ref, d_ref):
    L = q_ref.shape[0]
    half = RET_DK // 2
    scale = RET_DK ** -0.5

    @pl.when((pl.program_id(0) == 0) & (pl.program_id(1) == 0))
    def _():
        li = lax.broadcasted_iota(jnp.int32, (L, L), 0)
        si = lax.broadcasted_iota(jnp.int32, (L, L), 1)
        rel = (li - si).astype(F32)
        for h in range(RET_HEADS):
            d_ref[h] = jnp.where(si <= li, jnp.exp(rel * _ret_log_gamma(h)), 0.0) * scale

    @pl.when(pl.program_id(1) == 0)
    def _():
        r_ref[...] = jnp.zeros_like(r_ref)

    ang_t = invf_ref[...] * pos_ref[...].astype(F32)
    cos = jnp.cos(ang_t).T
    sin = jnp.sin(ang_t).T
    p_c = lax.broadcasted_iota(jnp.int32, (L, 1), 0).astype(F32)

    def rot(t):
        t1 = t[:, :half].astype(F32)
        t2 = t[:, half:].astype(F32)
        return jnp.concatenate([t1 * cos - t2 * sin, t1 * sin + t2 * cos], axis=1)

    for h in range(RET_HEADS):
        lg = _ret_log_gamma(h)
        sl = slice(h * RET_DK, (h + 1) * RET_DK)
        qr = rot(q_ref[:, sl])
        kr = rot(k_ref[:, sl])
        v = v_ref[:, h * RET_DV:(h + 1) * RET_DV]
        qb = qr.astype(BF16)
        r_mat = r_ref[h]
        s = _dot_nt(qb, kr.astype(BF16)) * d_ref[h]
        intra = _dot(s.astype(BF16), v)
        inter = _dot(qb, r_mat.astype(BF16)) * (jnp.exp((p_c + 1.0) * lg) * scale)
        kw = kr * jnp.exp((L - 1.0 - p_c) * lg)
        r_ref[h] = math.exp(L * lg) * r_mat + _dot_tn(kw.astype(BF16), v)
        y = _head_rms(intra + inter) * _silu(g_ref[:, h * RET_DV:(h + 1) * RET_DV].astype(F32))
        o_ref[:, h * RET_DV:(h + 1) * RET_DV] = y.astype(o_ref.dtype)


def _retention(proj, positions, *, batch, seq):
    m = proj.shape[0]
    L = L0_CHUNK
    nc = seq // L
    tok = lambda b, c: b * nc + c
    half = RET_DK // 2
    inv_freq = ROPE_BASE ** (-jnp.arange(0, RET_DK, 2, dtype=F32) / RET_DK)
    base = (2 * MLSTM_QK_W + 2 * MLSTM_V_W) // RET_QK_W
    return pl.pallas_call(
        _retention_kernel,
        grid=(batch, nc),
        in_specs=[
            pl.BlockSpec((L, RET_QK_W), lambda b, c: (tok(b, c), base)),
            pl.BlockSpec((L, RET_QK_W), lambda b, c: (tok(b, c), base + 1)),
            pl.BlockSpec((L, RET_V_W), lambda b, c: (tok(b, c), base + 2)),
            pl.BlockSpec((L, RET_V_W), lambda b, c: (tok(b, c), base + 3)),
            pl.BlockSpec((None, 1, L), lambda b, c: (tok(b, c), 0, 0)),
            pl.BlockSpec((half, 1), lambda b, c: (0, 0)),
        ],
        out_specs=pl.BlockSpec((L, RET_V_W), lambda b, c: (tok(b, c), 0)),
        out_shape=jax.ShapeDtypeStruct((m, RET_V_W), BF16),
        scratch_shapes=[
            pltpu.VMEM((RET_HEADS, RET_DK, RET_DV), F32),
            pltpu.VMEM((RET_HEADS, L, L), F32),
        ],
        compiler_params=pltpu.CompilerParams(
            dimension_semantics=("arbitrary", "arbitrary"), vmem_limit_bytes=VMEM_LIMIT),
        name="retention",
    )(proj, proj, proj, proj, positions.reshape(batch * nc, 1, L), inv_freq.reshape(half, 1))


def _gdn_kernel(q_ref, k_ref, v_ref, z_ref, grow_ref, norm_ref, o_ref, s_ref):
    T = q_ref.shape[0]
    C = GDN_CHUNK
    R = GDN_REP * C
    nch = T // C
    assert GDN_REP == 2 and R == LANES

    @pl.when(pl.program_id(2) == 0)
    def _():
        s_ref[...] = jnp.zeros_like(s_ref)

    rows = grow_ref[...]
    cols = _rows_to_cols(rows)

    ri = lax.broadcasted_iota(jnp.int32, (R, R), 0)
    ci = lax.broadcasted_iota(jnp.int32, (R, R), 1)
    same = (ri // C) == (ci // C)
    causal = same & (ci <= ri)
    strict = same & (ci < ri)
    eye = jnp.where(ri == ci, 1.0, 0.0)
    top = lax.broadcasted_iota(jnp.int32, (R, GDN_DV), 0) < C

    def stack2(a, b):
        return jnp.concatenate([a, b], axis=0)

    pre = {}

    def prepare(c, h):
        rs = slice(c * C, (c + 1) * C)
        ks = slice(h * GDN_DK, (h + 1) * GDN_DK)
        q_c = q_ref[rs, ks]
        k_c = k_ref[rs, ks]
        k2b = stack2(k_c, k_c)
        q2b = stack2(q_c, q_c)
        k2 = k2b.astype(F32)
        q2 = q2b.astype(F32)
        base = h * SUBLANES
        beta2 = stack2(cols[rs, base:base + 1], cols[rs, base + 1:base + 2])
        gc2 = stack2(cols[rs, base + GDN_REP:base + GDN_REP + 1], cols[rs, base + GDN_REP + 1:base + GDN_REP + 2])
        gc2_r = jnp.concatenate([rows[base + GDN_REP:base + GDN_REP + 1, rs],
                                 rows[base + GDN_REP + 1:base + GDN_REP + 2, rs]], axis=1)
        gl0 = gc2[C - 1:C, :]
        gl1 = gc2[R - 1:R, :]
        bt_w = jnp.broadcast_to(beta2, (R, LANES))
        gc_w = jnp.broadcast_to(gc2, (R, LANES))
        gl_w = jnp.where(top, gl0, gl1)
        eg_w = jnp.exp(gc_w)
        decay = jnp.exp(jnp.where(causal, gc_w - gc2_r, -jnp.inf))
        kq = _dot_nt(jnp.concatenate([k2b, q2b], axis=0), k2b)
        kk = kq[:R]
        qk = kq[R:]
        v2 = stack2(v_ref[rs, (2 * h) * GDN_DV:(2 * h + 1) * GDN_DV],
                    v_ref[rs, (2 * h + 1) * GDN_DV:(2 * h + 2) * GDN_DV]).astype(F32)
        pre[(c, h)] = dict(
            a=jnp.where(strict, bt_w * kk * decay, 0.0),
            attn=(qk * decay).astype(BF16),
            rhs=jnp.concatenate([v2 * bt_w, k2 * (bt_w * eg_w)], axis=1).astype(BF16),
            qg=(q2 * eg_w).astype(BF16),
            kt=(k2 * jnp.exp(gl_w - gc_w)).astype(BF16),
            gt=jnp.concatenate([jnp.broadcast_to(jnp.exp(gl0), (1, GDN_DV)),
                                jnp.broadcast_to(jnp.exp(gl1), (1, GDN_DV))], axis=1),
        )

    def solve(pairs):
        p = {}
        t = {}
        for key in pairs:
            p0 = -pre[key]["a"]
            pb = p0.astype(BF16)
            t[key] = eye + p0
            p[key] = _dot(pb, pb)
        step = 4
        while step < C:
            for key in pairs:
                pb = p[key].astype(BF16)
                x = _dot(jnp.concatenate([t[key].astype(BF16), pb], axis=0), pb)
                t[key] = t[key] + x[:R]
                p[key] = x[R:]
            step *= 2
        for key in pairs:
            t[key] = t[key] + _dot(t[key].astype(BF16), p[key].astype(BF16))
        return {key: _dot(t[key].astype(BF16), pre[key]["rhs"]) for key in pairs}

    s2 = [s_ref[h] for h in range(GDN_GROUP)]

    def recur(c, uw):
        rs = slice(c * C, (c + 1) * C)
        x1 = []
        for h in range(GDN_GROUP):
            w2b = uw[(c, h)][:, GDN_DV:].astype(BF16)
            lhs = jnp.concatenate([w2b, pre[(c, h)]["qg"]], axis=0)
            x1.append(_dot(lhs, s2[h].astype(BF16)))
        for h in range(GDN_GROUP):
            d = pre[(c, h)]
            ws = jnp.where(top, x1[h][:R, :GDN_DV], x1[h][:R, GDN_DV:])
            qs = jnp.where(top, x1[h][R:, :GDN_DV], x1[h][R:, GDN_DV:])
            vn = uw[(c, h)][:, :GDN_DV] - ws
            out = qs + _dot(d["attn"], vn.astype(BF16))
            vx = jnp.concatenate([jnp.where(top, vn, 0.0), jnp.where(top, 0.0, vn)], axis=1).astype(BF16)
            s2[h] = s2[h] * d["gt"] + _dot_tn(d["kt"], vx)
            z2 = stack2(z_ref[rs, (2 * h) * GDN_DV:(2 * h + 1) * GDN_DV],
                        z_ref[rs, (2 * h + 1) * GDN_DV:(2 * h + 2) * GDN_DV]).astype(F32)
            y = (_head_rms(out) * norm_ref[...] * z2).astype(o_ref.dtype)
            o_ref[rs, (2 * h) * GDN_DV:(2 * h + 1) * GDN_DV] = y[:C]
            o_ref[rs, (2 * h + 1) * GDN_DV:(2 * h + 2) * GDN_DV] = y[C:]

    groups = [[(c, h) for c in range(c0, c0 + GDN_PHASE_CHUNKS) for h in range(GDN_GROUP)]
              for c0 in range(0, nch, GDN_PHASE_CHUNKS)]
    for key in groups[0]:
        prepare(*key)
    uw = solve(groups[0])
    for gi, group in enumerate(groups):
        nxt = groups[gi + 1] if gi + 1 < len(groups) else []
        chunks = sorted({c for c, _ in group})
        recur(chunks[0], uw)
        for key in nxt:
            prepare(*key)
        for c in chunks[1:]:
            recur(c, uw)
        if nxt:
            uw = solve(nxt)
    for h in range(GDN_GROUP):
        s_ref[h] = s2[h]


def _gdn(proj, grow, norm_w, *, batch, seq):
    m = proj.shape[0]
    T = GDN_BLOCK
    G = GDN_GROUP
    nt = seq // T
    ng = GDN_QK_HEADS // G
    qw = G * GDN_DK
    vw = G * GDN_REP * GDN_DV
    tok = lambda b, h, t: b * nt + t
    kbase = GDN_QK_W // qw
    vbase = 2 * GDN_QK_W // vw
    zbase = (2 * GDN_QK_W + GDN_V_W) // vw
    return pl.pallas_call(
        _gdn_kernel,
        grid=(batch, ng, nt),
        in_specs=[
            pl.BlockSpec((T, qw), lambda b, h, t: (tok(b, h, t), h)),
            pl.BlockSpec((T, qw), lambda b, h, t: (tok(b, h, t), kbase + h)),
            pl.BlockSpec((T, vw), lambda b, h, t: (tok(b, h, t), vbase + h)),
            pl.BlockSpec((T, vw), lambda b, h, t: (tok(b, h, t), zbase + h)),
            pl.BlockSpec((G * SUBLANES, T), lambda b, h, t: (h, tok(b, h, t))),
            pl.BlockSpec((1, GDN_DV), lambda b, h, t: (0, 0)),
        ],
        out_specs=pl.BlockSpec((T, vw), lambda b, h, t: (tok(b, h, t), h)),
        out_shape=jax.ShapeDtypeStruct((m, GDN_V_W), BF16),
        scratch_shapes=[pltpu.VMEM((G, GDN_DK, GDN_REP * GDN_DV), F32)],
        compiler_params=pltpu.CompilerParams(
            dimension_semantics=("arbitrary", "arbitrary", "arbitrary"), vmem_limit_bytes=VMEM_LIMIT),
        name="gdn",
    )(proj, proj, proj, proj, grow, norm_w.reshape(1, GDN_DV))


def _l0_weights(w_in):
    d = w_in.shape[0]
    o = np.cumsum([0, MLSTM_QK_W, MLSTM_QK_W, MLSTM_V_W, MLSTM_V_W, 2 * MLSTM_HEADS,
                   RET_QK_W, RET_QK_W, RET_V_W, RET_V_W])
    src = np.concatenate([h * RET_DK + np.concatenate([np.arange(0, RET_DK, 2), np.arange(1, RET_DK, 2)])
                          for h in range(RET_HEADS)])
    perm = jnp.asarray(np.arange(RET_QK_W)[:, None] == src[None, :], dtype=BF16)

    def deinterleave(w):
        return jnp.dot(w, perm, preferred_element_type=BF16)

    wb = w_in.astype(BF16)
    w_main = jnp.concatenate([
        wb[:, o[0]:o[4]], deinterleave(wb[:, o[5]:o[6]]), deinterleave(wb[:, o[6]:o[7]]), wb[:, o[7]:o[9]],
    ], axis=1)
    return w_main, wb[:, o[4]:o[5]].T


def _l1_weights(w_in):
    d = w_in.shape[0]
    main_w = 2 * GDN_QK_W + 2 * GDN_V_W
    w_main = w_in.astype(BF16)
    b_pre =w_in[:, main_w:main_w + GDN_V_HEADS].T.reshape(GDN_QK_HEADS, GDN_REP, d)
    a_pre = w_in[:, main_w + GDN_V_HEADS:main_w + 2 * GDN_V_HEADS].T.reshape(GDN_QK_HEADS, GDN_REP, d)
    wgt = jnp.zeros((GDN_QK_HEADS, SUBLANES, d), F32)
    wgt = wgt.at[:, 0:GDN_REP].set(b_pre).at[:, GDN_REP:2 * GDN_REP].set(a_pre)
    return w_main, wgt.reshape(GATE_ROWS, d).astype(BF16)


def kernel(x, positions, l0_norm_mix, l0_w_in, l0_gate_bias, l0_mlstm_norm, l0_w_out, l0_norm_ffn, l0_w_gate,
           l0_w_up, l0_w_down, l1_norm_mix, l1_w_in, l1_conv_w, l1_a_log, l1_dt_bias, l1_gdn_norm, l1_w_out,
           l1_norm_ffn, l1_w_gate, l1_w_up, l1_w_down, final_norm):
    batch, seq, d = x.shape
    m = batch * seq
    x0 = x.reshape(m, d)

    w0, wgt0 = _l0_weights(l0_w_in)
    proj0, grow0 = _norm_proj(x0, l0_norm_mix, w0, wgt0, l0_gate_bias, tm=1024, tn=1024)
    h_m = _mlstm(proj0, grow0, l0_mlstm_norm, batch=batch, seq=seq)
    y_r = _retention(proj0, positions, batch=batch, seq=seq)
    x1 = _out_proj(x0, [h_m, y_r], l0_w_out.astype(BF16), tm=512)
    x1 = _ffn(x1, l0_norm_ffn, l0_w_gate.astype(BF16), l0_w_up.astype(BF16), l0_w_down.astype(BF16),
              final_norm, tm=512, tf=512, final_norm=False)

    w1, wgt1 = _l1_weights(l1_w_in)
    proj1, grow1 = _norm_proj_gdn(x1, l1_norm_mix, w1, wgt1, l1_conv_w, l1_a_log, l1_dt_bias,
                                  tm=1024, tn=1024, seq=seq)
    o = _gdn(proj1, grow1, l1_gdn_norm, batch=batch, seq=seq)
    x2 = _out_proj(x1, [o], l1_w_out.astype(BF16), tm=512)
    out = _ffn(x2, l1_norm_ffn, l1_w_gate.astype(BF16), l1_w_up.astype(BF16), l1_w_down.astype(BF16),
               final_norm, tm=512, tf=512, final_norm=True)
    return out.reshape(batch, seq, d)
```

```python
import functools
import math

import jax
import jax.numpy as jnp
import numpy as np
from jax import lax
from jax.experimental import pallas as pl
from jax.experimental.pallas import tpu as pltpu

F32 = jnp.float32
BF16 = jnp.bfloat16

D_MODEL = 2048
NORM_EPS = 1e-6
MLSTM_HEADS = 4
MLSTM_DQK = 128
MLSTM_DV = 256
GATE_SOFTCAP = 15.0
RET_HEADS = 4
RET_DK = 256
RET_DV = 256
ROPE_BASE = 10000.0
GDN_QK_HEADS = 16
GDN_V_HEADS = 32
GDN_DK = 128
GDN_DV = 128
GDN_REP = GDN_V_HEADS // GDN_QK_HEADS
CONV_WIDTH = 4
D_FF = -(-8 * D_MODEL // (3 * 256)) * 256

MLSTM_QK_W = MLSTM_HEADS * MLSTM_DQK
MLSTM_V_W = MLSTM_HEADS * MLSTM_DV
RET_QK_W = RET_HEADS * RET_DK
RET_V_W = RET_HEADS * RET_DV
GDN_QK_W = GDN_QK_HEADS * GDN_DK
GDN_V_W = GDN_V_HEADS * GDN_DV

LANES = 128
SUBLANES = 8
MXU_WIDTH = 256
PROJ_PIECE = MXU_WIDTH
GATE_ROWS = 128
VMEM_LIMIT = 56 * 1024 * 1024

L0_CHUNK = 256
GDN_CHUNK = 64
GDN_BLOCK = 256
GDN_GROUP = 8
GDN_PHASE_CHUNKS = 2

NT_DIMS = (((1,), (1,)), ((), ()))
TN_DIMS = (((0,), (0,)), ((), ()))


def _dot(a, b):
    return jnp.dot(a, b, preferred_element_type=F32)


def _dot_nt(a, b):
    return lax.dot_general(a, b, NT_DIMS, preferred_element_type=F32)


def _dot_tn(a, b):
    return lax.dot_general(a, b, TN_DIMS, preferred_element_type=F32)


def _sigmoid(x):
    return 0.5 * jnp.tanh(0.5 * x) + 0.5


def _silu(x):
    h = 0.5 * x
    return h + h * jnp.tanh(h)


def _softplus(x):
    return jnp.maximum(x, 0.0) + jnp.log1p(jnp.exp(-jnp.abs(x)))


def _log_sigmoid(x):
    return jnp.minimum(x, 0.0) - jnp.log1p(jnp.exp(-jnp.abs(x)))


def _cumsum_lanes(x, seg):
    pos = lax.broadcasted_iota(jnp.int32, x.shape, 1) % seg
    s = 1
    while s < seg:
        x = x + jnp.where(pos >= s, pltpu.roll(x, s, 1), 0.0)
        s *= 2
    return x


def _rows_to_cols(r):
    t = r.shape[1]
    if r.shape[0] < LANES:
        r = jnp.concatenate([r, jnp.zeros((LANES - r.shape[0], t), F32)], axis=0)
    return r.T


def _head_rms(y):
    return y * lax.rsqrt(jnp.mean(y * y, axis=-1, keepdims=True) + NORM_EPS)


def _norm_proj_kernel(x_ref, g_ref, w_ref, wgt_ref, bias_ref, o_ref, grow_ref, h_ref):
    @pl.when(pl.program_id(1) == 0)
    def _():
        x = x_ref[...]
        y = x * lax.rsqrt(jnp.mean(x * x, axis=-1, keepdims=True) + NORM_EPS)
        hb = (y * g_ref[...]).astype(BF16)
        h_ref[...] = hb
        g = _dot_nt(wgt_ref[...], hb) + bias_ref[...]
        g = GATE_SOFTCAP * jnp.tanh(g / GATE_SOFTCAP)
        row = lax.broadcasted_iota(jnp.int32, g.shape, 0)
        grow_ref[...] = jnp.where(row < MLSTM_HEADS, g, _cumsum_lanes(_log_sigmoid(g), L0_CHUNK))

    o_ref[...] = _dot(h_ref[...], w_ref[...]).astype(o_ref.dtype)


def _norm_proj(x2d, gain, w, wgt, gate_bias, *, tm, tn):
    m, d = x2d.shape
    n = w.shape[1]
    rows = 2 * MLSTM_HEADS
    assert m % tm == 0 and n % tn == 0 and tm % L0_CHUNK == 0 and wgt.shape == (rows, d)
    return pl.pallas_call(
        _norm_proj_kernel,
        grid=(m // tm, n // tn),
        in_specs=[
            pl.BlockSpec((tm, d), lambda i, j: (i, 0)),
            pl.BlockSpec((1, d), lambda i, j: (0, 0)),
            pl.BlockSpec((d, tn), lambda i, j: (0, j)),
            pl.BlockSpec((rows, d), lambda i, j: (0, 0)),
            pl.BlockSpec((rows, 1), lambda i, j: (0, 0)),
        ],
        out_specs=[
            pl.BlockSpec((tm, tn), lambda i, j: (i, j)),
            pl.BlockSpec((rows, tm), lambda i, j: (0, i)),
        ],
        out_shape=[
            jax.ShapeDtypeStruct((m, n), BF16),
            jax.ShapeDtypeStruct((rows, m), F32),
        ],
        scratch_shapes=[pltpu.VMEM((tm, d), BF16)],
        compiler_params=pltpu.CompilerParams(
            dimension_semantics=("arbitrary", "arbitrary"), vmem_limit_bytes=VMEM_LIMIT),
        name="norm_proj",
    )(x2d, gain.reshape(1, d), w, wgt, gate_bias.astype(F32).reshape(rows, 1))


def _norm_proj_gdn_kernel(x_ref, g_ref, w_ref, wgt_ref, cw_ref, alog_ref, dtb_ref, o_ref, grow_ref, h_ref, carry_ref,
                          *stage_refs, seq_tiles):
    i = pl.program_id(0)
    j = pl.program_id(1)
    tm, tn = o_ref.shape
    pad = SUBLANES
    n_q = GDN_QK_W // tn
    n_qk = 2 * GDN_QK_W // tn
    n_conv = (2 * GDN_QK_W + GDN_V_W) // tn

    @pl.when(j == 0)
    def _():
        x = x_ref[...]
        y = x * lax.rsqrt(jnp.mean(x * x, axis=-1, keepdims=True) + NORM_EPS)
        hb = (y * g_ref[...]).astype(BF16)
        h_ref[...] = hb
        g = _dot_nt(wgt_ref[...], hb)
        row = lax.broadcasted_iota(jnp.int32, g.shape, 0) % SUBLANES
        log_alpha = -jnp.exp(alog_ref[...]) * _softplus(g + dtb_ref[...])
        grow_ref[...] = jnp.where(row < GDN_REP, _sigmoid(g), _cumsum_lanes(log_alpha, GDN_CHUNK))

    pieces = [slice(c, c + PROJ_PIECE) for c in range(0, tn, PROJ_PIECE)]

    def conv_rows(t, cs):
        t1 = pltpu.roll(t, 1, 0)
        w = [cw_ref[s:s + 1, cs] for s in range(CONV_WIDTH)]
        assert CONV_WIDTH == 4
        return (w[3] * t + w[2] * t1) + pltpu.roll(w[1] * t + w[0] * t1, 2, 0)

    def stage_products():
        for cs, st in zip(pieces, stage_refs):
            st[...] = _dot(h_ref[...], w_ref[:, cs])

    def conv_silu(cs, st):
        r = st[...]
        prev = jnp.where(i % seq_tiles == 0, 0.0, carry_ref[j, :, cs])
        carry_ref[j, :, cs] = r[tm - pad:tm, :]
        top = conv_rows(jnp.concatenate([prev, r[0:pad, :]], axis=0), cs)[pad:2 * pad, :]
        return _silu(jnp.concatenate([top, conv_rows(r, cs)[pad:, :]], axis=0))

    @pl.when(j < n_qk)
    def _():
        sc = jnp.where(j < n_q, GDN_DK ** -0.5, 1.0)
        stage_products()
        for cs, st in zip(pieces, stage_refs):
            y = conv_silu(cs, st)
            for hh in range(PROJ_PIECE // GDN_DK):
                yh = y[:, hh * GDN_DK:(hh + 1) * GDN_DK]
                inv = lax.rsqrt(jnp.sum(yh * yh, axis=-1, keepdims=True) + NORM_EPS) * sc
                o_ref[:, cs.start + hh * GDN_DK:cs.start + (hh + 1) * GDN_DK] = (yh * inv).astype(o_ref.dtype)

    @pl.when((j >= n_qk) & (j < n_conv))
    def _():
        stage_products()
        for cs, st in zip(pieces, stage_refs):
            o_ref[:, cs] = conv_silu(cs, st).astype(o_ref.dtype)

    @pl.when(j >= n_conv)
    def _():
        for cs in pieces:
            o_ref[:, cs] = _silu(_dot(h_ref[...], w_ref[:, cs])).astype(o_ref.dtype)


def _norm_proj_gdn(x2d, gain, w, wgt, conv_w, a_log, dt_bias, *, tm, tn, seq):
    m, d = x2d.shape
    n = 2 * GDN_QK_W + 2 * GDN_V_W
    conv_c = conv_w.shape[1]
    assert m % tm == 0 and n % tn == 0 and seq % tm == 0 and tm % GDN_CHUNK == 0
    assert GDN_QK_W % tn == 0 and conv_c % tn == 0 and tn % GDN_DK == 0
    n_conv = conv_c // tn

    def per_head_rows(p):
        t = jnp.zeros((GDN_QK_HEADS, SUBLANES), F32)
        t = t.at[:, GDN_REP:2 * GDN_REP].set(p.astype(F32).reshape(GDN_QK_HEADS, GDN_REP))
        return t.reshape(GATE_ROWS, 1)

    return pl.pallas_call(
        functools.partial(_norm_proj_gdn_kernel, seq_tiles=seq // tm),
        grid=(m // tm, n // tn),
        in_specs=[
            pl.BlockSpec((tm, d), lambda i, j: (i, 0)),
            pl.BlockSpec((1, d), lambda i, j: (0, 0)),
            pl.BlockSpec((d, tn), lambda i, j: (0, j)),
            pl.BlockSpec((GATE_ROWS, d), lambda i, j: (0, 0)),
            pl.BlockSpec((CONV_WIDTH, tn), lambda i, j: (0, jnp.minimum(j, n_conv - 1))),
            pl.BlockSpec((GATE_ROWS, 1), lambda i, j: (0, 0)),
            pl.BlockSpec((GATE_ROWS, 1), lambda i, j: (0, 0)),
        ],
        out_specs=[
            pl.BlockSpec((tm, tn), lambda i, j: (i, j)),
            pl.BlockSpec((GATE_ROWS, tm), lambda i, j: (0, i)),
        ],
        out_shape=[
            jax.ShapeDtypeStruct((m, n), BF16),
            jax.ShapeDtypeStruct((GATE_ROWS, m), F32),
        ],
        scratch_shapes=[
            pltpu.VMEM((tm, d), BF16),
            pltpu.VMEM((n_conv, SUBLANES, tn), F32),
        ] + [pltpu.VMEM((tm, PROJ_PIECE), F32)] * (tn // PROJ_PIECE),
        compiler_params=pltpu.CompilerParams(
            dimension_semantics=("arbitrary", "arbitrary"), vmem_limit_bytes=VMEM_LIMIT),
        name="norm_proj_gdn",
    )(x2d, gain.reshape(1, d), w, wgt, conv_w, per_head_rows(a_log), per_head_rows(dt_bias))


def _out_proj_kernel(x_ref, *refs):
    o_ref = refs[-1]
    n_in = (len(refs) - 1) // 2
    acc = x_ref[...]
    for i in range(n_in):
        acc = acc + _dot(refs[i][...], refs[n_in + i][...])
    o_ref[...] = acc


def _out_proj(x2d, acts, w, *, tm):
    m, d = x2d.shape
    assert m % tm == 0 and sum(a.shape[1] for a in acts) == w.shape[0]
    offs = np.cumsum([0] + [a.shape[1] for a in acts])
    w_specs = []
    for a, off in zip(acts, offs):
        ka = a.shape[1]
        assert off % ka == 0
        w_specs.append(pl.BlockSpec((ka, d), functools.partial(lambda i, blk: (blk, 0), blk=int(off // ka)),
                                    pipeline_mode=pl.Buffered(1)))
    return pl.pallas_call(
        _out_proj_kernel,
        grid=(m // tm,),
        in_specs=(
            [pl.BlockSpec((tm, d), lambda i: (i, 0))]
            + [pl.BlockSpec((tm, a.shape[1]), lambda i: (i, 0)) for a in acts]
            + w_specs
        ),
        out_specs=pl.BlockSpec((tm, d), lambda i: (i, 0)),
        out_shape=jax.ShapeDtypeStruct((m, d), F32),
        compiler_params=pltpu.CompilerParams(
            dimension_semantics=("arbitrary",), vmem_limit_bytes=VMEM_LIMIT),
        name="out_proj",
    )(x2d, *acts, *([w] * len(acts)))


def _ffn_kernel(x_ref, g_ref, wg_ref, wu_ref, wd_ref, fg_ref, o_ref, h_ref, *, final_norm):
    j = pl.program_id(1)

    @pl.when(j == 0)
    def _():
        x = x_ref[...]
        y = x * lax.rsqrt(jnp.mean(x * x, axis=-1, keepdims=True) + NORM_EPS)
        h_ref[...] = (y * g_ref[...]).astype(BF16)
        o_ref[...] = x

    hb = h_ref[...]
    a = _dot(hb, wg_ref[...])
    b = _dot(hb, wu_ref[...])
    o_ref[...] += _dot((_silu(a) * b).astype(BF16), wd_ref[...])

    if final_norm:
        @pl.when(j == pl.num_programs(1) - 1)
        def _():
            x = o_ref[...]
            y = x * lax.rsqrt(jnp.mean(x * x, axis=-1, keepdims=True) + NORM_EPS)
            o_ref[...] = y * fg_ref[...]


def _ffn(x2d, gain, wg, wu, wd, final_gain, *, tm, tf, final_norm):
    m, d = x2d.shape
    f = wg.shape[1]
    assert m % tm == 0 and f % tf == 0
    return pl.pallas_call(
        functools.partial(_ffn_kernel, final_norm=final_norm),
        grid=(m // tm, f // tf),
        in_specs=[
            pl.BlockSpec((tm, d), lambda i, j: (i, 0)),
            pl.BlockSpec((1, d), lambda i, j: (0, 0)),
            pl.BlockSpec((d, tf), lambda i, j: (0, j)),
            pl.BlockSpec((d, tf), lambda i, j: (0, j)),
            pl.BlockSpec((tf, d), lambda i, j: (j, 0)),
            pl.BlockSpec((1, d), lambda i, j: (0, 0)),
        ],
        out_specs=pl.BlockSpec((tm, d), lambda i, j: (i, 0)),
        out_shape=jax.ShapeDtypeStruct((m, d), F32),
        scratch_shapes=[pltpu.VMEM((tm, d), BF16)],
        compiler_params=pltpu.CompilerParams(
            dimension_semantics=("arbitrary", "arbitrary"), vmem_limit_bytes=VMEM_LIMIT),
        name="ffn_final" if final_norm else "ffn",
    )(x2d, gain.reshape(1, d), wg, wu, wd, final_gain.reshape(1, d))


def _mlstm_kernel(q_ref, k_ref, v_ref, og_ref, grow_ref, norm_ref, o_ref, c_ref, n_ref, m_ref):
    L = q_ref.shape[0]
    nh = MLSTM_HEADS

    @pl.when(pl.program_id(1) == 0)
    def _():
        c_ref[...] = jnp.zeros_like(c_ref)
        n_ref[...] = jnp.zeros_like(n_ref)
        m_ref[...] = jnp.full_like(m_ref, -jnp.inf)

    rows = grow_ref[...]
    cols = _rows_to_cols(rows)

    li = lax.broadcasted_iota(jnp.int32, (L, L), 0)
    si = lax.broadcasted_iota(jnp.int32, (L, L), 1)
    causal = si <= li
    scale = MLSTM_DQK ** -0.5

    st = []
    for h in range(nh):
        li_r = rows[h:h + 1, :]
        bc_r = rows[nh + h:nh + h + 1, :]
        li_c = cols[:, h:h + 1]
        bc_c = cols[:, nh + h:nh + h + 1]
        g_tot = bc_c[L - 1:L, :]
        m_prev = m_ref[h][:, 0:1]
        log_d = jnp.where(causal, bc_c + (li_r - bc_r), -jnp.inf)
        m_inter = bc_c + m_prev
        m_t = jnp.maximum(m_inter, jnp.max(log_d, axis=-1, keepdims=True))
        q = q_ref[:, h * MLSTM_DQK:(h + 1) * MLSTM_DQK]
        k = k_ref[:, h * MLSTM_DQK:(h + 1) * MLSTM_DQK]
        st.append(dict(
            q=q, k=k, v=v_ref[:, h * MLSTM_DV:(h + 1) * MLSTM_DV], c_mat=c_ref[h], n_vec=n_ref[h],
            m_t=m_t, inter=jnp.exp(m_inter - m_t) * scale,
            s=_dot_nt(q, k) * scale * jnp.exp(log_d - m_t),
            log_w=g_tot + (li_c - bc_c), g_tot=g_tot, m_prev=m_prev))

    for d in st:
        num = _dot(d["s"].astype(BF16), d["v"]) + d["inter"] * _dot(d["q"], d["c_mat"].astype(BF16))
        qn = jnp.sum(d["q"].astype(F32) * d["n_vec"], axis=-1, keepdims=True)
        den = jnp.sum(d["s"], axis=-1, keepdims=True) + d["inter"] * qn
        d["hout"] = num * (1.0 / jnp.maximum(jnp.abs(den), jnp.exp(-d["m_t"])))

    for h, d in enumerate(st):
        y = _head_rms(d["hout"]) * norm_ref[:, h * MLSTM_DV:(h + 1) * MLSTM_DV]
        og = og_ref[:, h * MLSTM_DV:(h + 1) * MLSTM_DV].astype(F32)
        o_ref[:, h * MLSTM_DV:(h + 1) * MLSTM_DV] = (_sigmoid(og) * y).astype(o_ref.dtype)

    for h, d in enumerate(st):
        m_w = jnp.max(d["log_w"], axis=0, keepdims=True)
        m_new = jnp.maximum(d["g_tot"] + d["m_prev"], m_w)
        decay = jnp.exp(d["g_tot"] + d["m_prev"] - m_new)
        kw = d["k"].astype(F32) * jnp.exp(d["log_w"] - m_new)
        c_ref[h] = decay * d["c_mat"] + _dot_tn(kw.astype(BF16), d["v"])
        n_ref[h] = decay * d["n_vec"] + jnp.sum(kw, axis=0, keepdims=True)
        m_ref[h] = jnp.broadcast_to(m_new, (1, LANES))


def _mlstm(proj, grow, norm_w, *, batch, seq):
    m = proj.shape[0]
    L = L0_CHUNK
    nc = seq // L
    tok = lambda b, c: b * nc + c
    return pl.pallas_call(
        _mlstm_kernel,
        grid=(batch, nc),
        in_specs=[
            pl.BlockSpec((L, MLSTM_QK_W), lambda b, c: (tok(b, c), 0)),
            pl.BlockSpec((L, MLSTM_QK_W), lambda b, c: (tok(b, c), 1)),
            pl.BlockSpec((L, MLSTM_V_W), lambda b, c: (tok(b, c), 1)),
            pl.BlockSpec((L, MLSTM_V_W), lambda b, c: (tok(b, c), 2)),
            pl.BlockSpec((SUBLANES, L), lambda b, c: (0, tok(b, c))),
            pl.BlockSpec((1, MLSTM_V_W), lambda b, c: (0, 0)),
        ],
        out_specs=pl.BlockSpec((L, MLSTM_V_W), lambda b, c: (tok(b, c), 0)),
        out_shape=jax.ShapeDtypeStruct((m, MLSTM_V_W), BF16),
        scratch_shapes=[
            pltpu.VMEM((MLSTM_HEADS, MLSTM_DQK, MLSTM_DV), F32),
            pltpu.VMEM((MLSTM_HEADS, 1, MLSTM_DQK), F32),
            pltpu.VMEM((MLSTM_HEADS, 1, LANES), F32),
        ],
        compiler_params=pltpu.CompilerParams(
            dimension_semantics=("arbitrary", "arbitrary"), vmem_limit_bytes=VMEM_LIMIT),
        name="mlstm",
    )(proj, proj, proj, proj, grow, norm_w.reshape(1, MLSTM_V_W))


def _ret_log_gamma(h):
    return math.log(1.0 - 2.0 ** (-5.0 - h))


def _retention_kernel(q_ref, k_ref, v_ref, g_ref, pos_ref, invf_ref, o_ref, r_ref, d_ref):
    L = q_ref.shape[0]
    half = RET_DK // 2
    scale = RET_DK ** -0.5

    @pl.when((pl.program_id(0) == 0) & (pl.program_id(1) == 0))
    def _():
        li = lax.broadcasted_iota(jnp.int32, (L, L), 0)
        si = lax.broadcasted_iota(jnp.int32, (L, L), 1)
        rel = (li - si).astype(F32)
        for h in range(RET_HEADS):
            d_ref[h] = jnp.where(si <= li, jnp.exp(rel * _ret_log_gamma(h)), 0.0) * scale

    @pl.when(pl.program_id(1) == 0)
    def _():
        r_ref[...] = jnp.zeros_like(r_ref)

    ang_t = invf_ref[...] * pos_ref[...].astype(F32)
    cos = jnp.cos(ang_t).T
    sin = jnp.sin(ang_t).T
    p_c = lax.broadcasted_iota(jnp.int32, (L, 1), 0).astype(F32)

    def rot(t):
        t1 = t[:, :half].astype(F32)
        t2 = t[:, half:].astype(F32)
        return jnp.concatenate([t1 * cos - t2 * sin, t1 * sin + t2 * cos], axis=1)

    for h in range(RET_HEADS):
        lg = _ret_log_gamma(h)
        sl = slice(h * RET_DK, (h + 1) * RET_DK)
        qr = rot(q_ref[:, sl])
        kr = rot(k_ref[:, sl])
        v = v_ref[:, h * RET_DV:(h + 1) * RET_DV]
        qb = qr.astype(BF16)
        r_mat = r_ref[h]
        s = _dot_nt(qb, kr.astype(BF16)) * d_ref[h]
        intra = _dot(s.astype(BF16), v)
        inter = _dot(qb, r_mat.astype(BF16)) * (jnp.exp((p_c + 1.0) * lg) * scale)
        kw = kr * jnp.exp((L - 1.0 - p_c) * lg)
        r_ref[h] = math.exp(L * lg) * r_mat + _dot_tn(kw.astype(BF16), v)
        y = _head_rms(intra + inter) * _silu(g_ref[:, h * RET_DV:(h + 1) * RET_DV].astype(F32))
        o_ref[:, h * RET_DV:(h + 1) * RET_DV] = y.astype(o_ref.dtype)


def _retention(proj, positions, *, batch, seq):
    m = proj.shape[0]
    L = L0_CHUNK
    nc = seq // L
    tok = lambda b, c: b * nc + c
    half = RET_DK // 2
    inv_freq = ROPE_BASE ** (-jnp.arange(0, RET_DK, 2, dtype=F32) / RET_DK)
    base = (2 * MLSTM_QK_W + 2 * MLSTM_V_W) // RET_QK_W
    return pl.pallas_call(
        _retention_kernel,
        grid=(batch, nc),
        in_specs=[
            pl.BlockSpec((L, RET_QK_W), lambda b, c: (tok(b, c), base)),
            pl.BlockSpec((L, RET_QK_W), lambda b, c: (tok(b, c), base + 1)),
            pl.BlockSpec((L, RET_V_W), lambda b, c: (tok(b, c), base + 2)),
            pl.BlockSpec((L, RET_V_W), lambda b, c: (tok(b, c), base + 3)),
            pl.BlockSpec((None, 1, L), lambda b, c: (tok(b, c), 0, 0)),
            pl.BlockSpec((half, 1), lambda b, c: (0, 0)),
        ],
        out_specs=pl.BlockSpec((L, RET_V_W), lambda b, c: (tok(b, c), 0)),
        out_shape=jax.ShapeDtypeStruct((m, RET_V_W), BF16),
        scratch_shapes=[
            pltpu.VMEM((RET_HEADS, RET_DK, RET_DV), F32),
            pltpu.VMEM((RET_HEADS, L, L), F32),
        ],
        compiler_params=pltpu.CompilerParams(
            dimension_semantics=("arbitrary", "arbitrary"), vmem_limit_bytes=VMEM_LIMIT),
        name="retention",
    )(proj, proj, proj, proj, positions.reshape(batch * nc, 1, L), inv_freq.reshape(half, 1))


def _gdn_kernel(q_ref, k_ref, v_ref, z_ref, grow_ref, norm_ref, o_ref, s_ref):
    T = q_ref.shape[0]
    C = GDN_CHUNK
    R = GDN_REP * C
    nch = T // C
    assert GDN_REP == 2 and R == LANES

    @pl.when(pl.program_id(2) == 0)
    def _():
        s_ref[...] = jnp.zeros_like(s_ref)

    rows = grow_ref[...]
    cols = _rows_to_cols(rows)

    ri = lax.broadcasted_iota(jnp.int32, (R, R), 0)
    ci = lax.broadcasted_iota(jnp.int32, (R, R), 1)
    same = (ri // C) == (ci // C)
    causal = same & (ci <= ri)
    strict = same & (ci < ri)
    eye = jnp.where(ri == ci, 1.0, 0.0)
    top = lax.broadcasted_iota(jnp.int32, (R, GDN_DV), 0) < C

    def stack2(a, b):
        return jnp.concatenate([a, b], axis=0)

    pre = {}

    def prepare(c, h):
        rs = slice(c * C, (c + 1) * C)
        ks = slice(h * GDN_DK, (h + 1) * GDN_DK)
        q_c = q_ref[rs, ks]
        k_c = k_ref[rs, ks]
        k2b = stack2(k_c, k_c)
        q2b = stack2(q_c, q_c)
        k2 = k2b.astype(F32)
        q2 = q2b.astype(F32)
        base = h * SUBLANES
        beta2 = stack2(cols[rs, base:base + 1], cols[rs, base + 1:base + 2])
        gc2 = stack2(cols[rs, base + GDN_REP:base + GDN_REP + 1], cols[rs, base + GDN_REP + 1:base + GDN_REP + 2])
        gc2_r = jnp.concatenate([rows[base + GDN_REP:base + GDN_REP + 1, rs],
                                 rows[base + GDN_REP + 1:base + GDN_REP + 2, rs]], axis=1)
        gl0 = gc2[C - 1:C, :]
        gl1 = gc2[R - 1:R, :]
        bt_w = jnp.broadcast_to(beta2, (R, LANES))
        gc_w = jnp.broadcast_to(gc2, (R, LANES))
        gl_w = jnp.where(top, gl0, gl1)
        eg_w = jnp.exp(gc_w)
        decay = jnp.exp(jnp.where(causal, gc_w - gc2_r, -jnp.inf))
        kq = _dot_nt(jnp.concatenate([k2b, q2b], axis=0), k2b)
        kk = kq[:R]
        qk = kq[R:]
        v2 = stack2(v_ref[rs, (2 * h) * GDN_DV:(2 * h + 1) * GDN_DV],
                    v_ref[rs, (2 * h + 1) * GDN_DV:(2 * h + 2) * GDN_DV]).astype(F32)
        pre[(c, h)] = dict(
            a=jnp.where(strict, bt_w * kk * decay, 0.0),
            attn=(qk * decay).astype(BF16),
            rhs=jnp.concatenate([v2 * bt_w, k2 * (bt_w * eg_w)], axis=1).astype(BF16),
            qg=(q2 * eg_w).astype(BF16),
            kt=(k2 * jnp.exp(gl_w - gc_w)).astype(BF16),
            gt=jnp.concatenate([jnp.broadcast_to(jnp.exp(gl0), (1, GDN_DV)),
                                jnp.broadcast_to(jnp.exp(gl1), (1, GDN_DV))], axis=1),
        )

    def solve(pairs):
        p = {}
        t = {}
        for key in pairs:
            p0 = -pre[key]["a"]
            pb = p0.astype(BF16)
            t[key] = eye + p0
            p[key] = _dot(pb, pb)
        step = 4
        while step < C:
            for key in pairs:
                pb = p[key].astype(BF16)
                x = _dot(jnp.concatenate([t[key].astype(BF16), pb], axis=0), pb)
                t[key] = t[key] + x[:R]
                p[key] = x[R:]
            step *= 2
        for key in pairs:
            t[key] = t[key] + _dot(t[key].astype(BF16), p[key].astype(BF16))
        return {key: _dot(t[key].astype(BF16), pre[key]["rhs"]) for key in pairs}

    s2 = [s_ref[h] for h in range(GDN_GROUP)]

    def recur(c, uw):
        rs = slice(c * C, (c + 1) * C)
        x1 = []
        for h in range(GDN_GROUP):
            w2b = uw[(c, h)][:, GDN_DV:].astype(BF16)
            lhs = jnp.concatenate([w2b, pre[(c, h)]["qg"]], axis=0)
            x1.append(_dot(lhs, s2[h].astype(BF16)))
        for h in range(GDN_GROUP):
            d = pre[(c, h)]
            ws = jnp.where(top, x1[h][:R, :GDN_DV], x1[h][:R, GDN_DV:])
            qs = jnp.where(top, x1[h][R:, :GDN_DV], x1[h][R:, GDN_DV:])
            vn = uw[(c, h)][:, :GDN_DV] - ws
            out = qs + _dot(d["attn"], vn.astype(BF16))
            vx = jnp.concatenate([jnp.where(top, vn, 0.0), jnp.where(top, 0.0, vn)], axis=1).astype(BF16)
            s2[h] = s2[h] * d["gt"] + _dot_tn(d["kt"], vx)
            z2 = stack2(z_ref[rs, (2 * h) * GDN_DV:(2 * h + 1) * GDN_DV],
                        z_ref[rs, (2 * h + 1) * GDN_DV:(2 * h + 2) * GDN_DV]).astype(F32)
            y = (_head_rms(out) * norm_ref[...] * z2).astype(o_ref.dtype)
            o_ref[rs, (2 * h) * GDN_DV:(2 * h + 1) * GDN_DV] = y[:C]
            o_ref[rs, (2 * h + 1) * GDN_DV:(2 * h + 2) * GDN_DV] = y[C:]

    groups = [[(c, h) for c in range(c0, c0 + GDN_PHASE_CHUNKS) for h in range(GDN_GROUP)]
              for c0 in range(0, nch, GDN_PHASE_CHUNKS)]
    for key in groups[0]:
        prepare(*key)
    uw = solve(groups[0])
    for gi, group in enumerate(groups):
        nxt = groups[gi + 1] if gi + 1 < len(groups) else []
        chunks = sorted({c for c, _ in group})
        recur(chunks[0], uw)
        for key in nxt:
            prepare(*key)
        for c in chunks[1:]:
            recur(c, uw)
        if nxt:
            uw = solve(nxt)
    for h in range(GDN_GROUP):
        s_ref[h] = s2[h]


def _gdn(proj, grow, norm_w, *, batch, seq):
    m = proj.shape[0]
    T = GDN_BLOCK
    G = GDN_GROUP
    nt = seq // T
    ng = GDN_QK_HEADS // G
    qw = G * GDN_DK
    vw = G * GDN_REP * GDN_DV
    tok = lambda b, h, t: b * nt + t
    kbase = GDN_QK_W // qw
    vbase = 2 * GDN_QK_W // vw
    zbase = (2 * GDN_QK_W + GDN_V_W) // vw
    return pl.pallas_call(
        _gdn_kernel,
        grid=(batch, ng, nt),
        in_specs=[
            pl.BlockSpec((T, qw), lambda b, h, t: (tok(b, h, t), h)),
            pl.BlockSpec((T, qw), lambda b, h, t: (tok(b, h, t), kbase + h)),
            pl.BlockSpec((T, vw), lambda b, h, t: (tok(b, h, t), vbase + h)),
            pl.BlockSpec((T, vw), lambda b, h, t: (tok(b, h, t), zbase + h)),
            pl.BlockSpec((G * SUBLANES, T), lambda b, h, t: (h, tok(b, h, t))),
            pl.BlockSpec((1, GDN_DV), lambda b, h, t: (0, 0)),
        ],
        out_specs=pl.BlockSpec((T, vw), lambda b, h, t: (tok(b, h, t), h)),
        out_shape=jax.ShapeDtypeStruct((m, GDN_V_W), BF16),
        scratch_shapes=[pltpu.VMEM((G, GDN_DK, GDN_REP * GDN_DV), F32)],
        compiler_params=pltpu.CompilerParams(
            dimension_semantics=("arbitrary", "arbitrary", "arbitrary"), vmem_limit_bytes=VMEM_LIMIT),
        name="gdn",
    )(proj, proj, proj, proj, grow, norm_w.reshape(1, GDN_DV))


def _l0_weights(w_in):
    d = w_in.shape[0]
    o = np.cumsum([0, MLSTM_QK_W, MLSTM_QK_W, MLSTM_V_W, MLSTM_V_W, 2 * MLSTM_HEADS,
                   RET_QK_W, RET_QK_W, RET_V_W, RET_V_W])
    src = np.concatenate([h * RET_DK + np.concatenate([np.arange(0, RET_DK, 2), np.arange(1, RET_DK, 2)])
                          for h in range(RET_HEADS)])
    perm = jnp.asarray(np.arange(RET_QK_W)[:, None] == src[None, :], dtype=BF16)

    def deinterleave(w):
        return jnp.dot(w, perm, preferred_element_type=BF16)

    wb = w_in.astype(BF16)
    w_main = jnp.concatenate([
        wb[:, o[0]:o[4]], deinterleave(wb[:, o[5]:o[6]]), deinterleave(wb[:, o[6]:o[7]]), wb[:, o[7]:o[9]],
    ], axis=1)
    return w_main, wb[:, o[4]:o[5]].T


def _l1_weights(w_in):
    d = w_in.shape[0]
    main_w = 2 * GDN_QK_W + 2 * GDN_V_W
    w_main = w_in.astype(BF16)
    b_pre =w_in[:, main_w:main_w + GDN_V_HEADS].T.reshape(GDN_QK_HEADS, GDN_REP, d)
    a_pre = w_in[:, main_w + GDN_V_HEADS:main_w + 2 * GDN_V_HEADS].T.reshape(GDN_QK_HEADS, GDN_REP, d)
    wgt = jnp.zeros((GDN_QK_HEADS, SUBLANES, d), F32)
    wgt = wgt.at[:, 0:GDN_REP].set(b_pre).at[:, GDN_REP:2 * GDN_REP].set(a_pre)
    return w_main, wgt.reshape(GATE_ROWS, d).astype(BF16)


def kernel(x, positions, l0_norm_mix, l0_w_in, l0_gate_bias, l0_mlstm_norm, l0_w_out, l0_norm_ffn, l0_w_gate,
           l0_w_up, l0_w_down, l1_norm_mix, l1_w_in, l1_conv_w, l1_a_log, l1_dt_bias, l1_gdn_norm, l1_w_out,
           l1_norm_ffn, l1_w_gate, l1_w_up, l1_w_down, final_norm):
    batch, seq, d = x.shape
    m = batch * seq
    x0 = x.reshape(m, d)

    w0, wgt0 = _l0_weights(l0_w_in)
    proj0, grow0 = _norm_proj(x0, l0_norm_mix, w0, wgt0, l0_gate_bias, tm=1024, tn=1024)
    h_m = _mlstm(proj0, grow0, l0_mlstm_norm, batch=batch, seq=seq)
    y_r = _retention(proj0, positions, batch=batch, seq=seq)
    x1 = _out_proj(x0, [h_m, y_r], l0_w_out.astype(BF16), tm=512)
    x1 = _ffn(x1, l0_norm_ffn, l0_w_gate.astype(BF16), l0_w_up.astype(BF16), l0_w_down.astype(BF16),
              final_norm, tm=512, tf=512, final_norm=False)

    w1, wgt1 = _l1_weights(l1_w_in)
    proj1, grow1 = _norm_proj_gdn(x1, l1_norm_mix, w1, wgt1, l1_conv_w, l1_a_log, l1_dt_bias,
                                  tm=1024, tn=1024, seq=seq)
    o = _gdn(proj1, grow1, l1_gdn_norm, batch=batch, seq=seq)
    x2 = _out_proj(x1, [o], l1_w_out.astype(BF16), tm=512)
    out = _ffn(x2, l1_norm_ffn, l1_w_gate.astype(BF16), l1_w_up.astype(BF16), l1_w_down.astype(BF16),
               final_norm, tm=512, tf=512, final_norm=True)
    return out.reshape(batch, seq, d)
```

```python
import functools
import math

import jax
import jax.numpy as jnp
import numpy as np
from jax import lax
from jax.experimental import pallas as pl
from jax.experimental.pallas import tpu as pltpu

F32 = jnp.float32
BF16 = jnp.bfloat16

D_MODEL = 2048
NORM_EPS = 1e-6
MLSTM_HEADS = 4
MLSTM_DQK = 128
MLSTM_DV = 256
GATE_SOFTCAP = 15.0
RET_HEADS = 4
RET_DK = 256
RET_DV = 256
ROPE_BASE = 10000.0
GDN_QK_HEADS = 16
GDN_V_HEADS = 32
GDN_DK = 128
GDN_DV = 128
GDN_REP = GDN_V_HEADS // GDN_QK_HEADS
CONV_WIDTH = 4
D_FF = -(-8 * D_MODEL // (3 * 256)) * 256

MLSTM_QK_W = MLSTM_HEADS * MLSTM_DQK
MLSTM_V_W = MLSTM_HEADS * MLSTM_DV
RET_QK_W = RET_HEADS * RET_DK
RET_V_W = RET_HEADS * RET_DV
GDN_QK_W = GDN_QK_HEADS * GDN_DK
GDN_V_W = GDN_V_HEADS * GDN_DV

LANES = 128
SUBLANES = 8
MXU_WIDTH = 256
PROJ_PIECE = MXU_WIDTH
GATE_ROWS = 128
VMEM_LIMIT = 56 * 1024 * 1024

L0_CHUNK = 512
RET_CHUNK = 256
GDN_CHUNK = 64
GDN_BLOCK = 512
GDN_GROUP = 8
GDN_PHASE_CHUNKS = 2

NT_DIMS = (((1,), (1,)), ((), ()))
TN_DIMS = (((0,), (0,)), ((), ()))


def _dot(a, b):
    return jnp.dot(a, b, preferred_element_type=F32)


def _dot_nt(a, b):
    return lax.dot_general(a, b, NT_DIMS, preferred_element_type=F32)


def _dot_tn(a, b):
    return lax.dot_general(a, b, TN_DIMS, preferred_element_type=F32)


def _sigmoid(x):
    return 0.5 * jnp.tanh(0.5 * x) + 0.5


def _silu(x):
    h = 0.5 * x
    return h + h * jnp.tanh(h)


def _softplus(x):
    return jnp.maximum(x, 0.0) + jnp.log1p(jnp.exp(-jnp.abs(x)))


def _log_sigmoid(x):
    return jnp.minimum(x, 0.0) - jnp.log1p(jnp.exp(-jnp.abs(x)))


def _cumsum_lanes(x, seg):
    pos = lax.broadcasted_iota(jnp.int32, x.shape, 1) % seg
    s = 1
    while s < seg:
        x = x + jnp.where(pos >= s, pltpu.roll(x, s, 1), 0.0)
        s *= 2
    return x


def _rows_to_cols(r):
    t = r.shape[1]
    if r.shape[0] < LANES:
        r = jnp.concatenate([r, jnp.zeros((LANES - r.shape[0], t), F32)], axis=0)
    return r.T


def _head_rms(y):
    return y * lax.rsqrt(jnp.mean(y * y, axis=-1, keepdims=True) + NORM_EPS)


def _norm_proj_kernel(x_ref, g_ref, w_ref, wgt_ref, bias_ref, o_ref, grow_ref, h_ref):
    @pl.when(pl.program_id(1) == 0)
    def _():
        x = x_ref[...]
        y = x * lax.rsqrt(jnp.mean(x * x, axis=-1, keepdims=True) + NORM_EPS)
        hb = (y * g_ref[...]).astype(BF16)
        h_ref[...] = hb
        g = _dot_nt(wgt_ref[...], hb) + bias_ref[...]
        g = GATE_SOFTCAP * jnp.tanh(g / GATE_SOFTCAP)
        row = lax.broadcasted_iota(jnp.int32, g.shape, 0)
        grow_ref[...] = jnp.where(row < MLSTM_HEADS, g, _cumsum_lanes(_log_sigmoid(g), L0_CHUNK))

    o_ref[...] = _dot(h_ref[...], w_ref[...]).astype(o_ref.dtype)


def _norm_proj(x2d, gain, w, wgt, gate_bias, *, tm, tn):
    m, d = x2d.shape
    n = w.shape[1]
    rows = 2 * MLSTM_HEADS
    assert m % tm == 0 and n % tn == 0 and tm % L0_CHUNK == 0 and wgt.shape == (rows, d)
    return pl.pallas_call(
        _norm_proj_kernel,
        grid=(m // tm, n // tn),
        in_specs=[
            pl.BlockSpec((tm, d), lambda i, j: (i, 0)),
            pl.BlockSpec((1, d), lambda i, j: (0, 0)),
            pl.BlockSpec((d, tn), lambda i, j: (0, j)),
            pl.BlockSpec((rows, d), lambda i, j: (0, 0)),
            pl.BlockSpec((rows, 1), lambda i, j: (0, 0)),
        ],
        out_specs=[
            pl.BlockSpec((tm, tn), lambda i, j: (i, j)),
            pl.BlockSpec((rows, tm), lambda i, j: (0, i)),
        ],
        out_shape=[
            jax.ShapeDtypeStruct((m, n), BF16),
            jax.ShapeDtypeStruct((rows, m), F32),
        ],
        scratch_shapes=[pltpu.VMEM((tm, d), BF16)],
        compiler_params=pltpu.CompilerParams(
            dimension_semantics=("arbitrary", "arbitrary"), vmem_limit_bytes=VMEM_LIMIT),
        name="norm_proj",
    )(x2d, gain.reshape(1, d), w, wgt, gate_bias.astype(F32).reshape(rows, 1))


def _norm_proj_gdn_kernel(x_ref, g_ref, w_ref, wgt_ref, cw_ref, alog_ref, dtb_ref, o_ref, grow_ref, h_ref, carry_ref,
                          *stage_refs, seq_tiles):
    i = pl.program_id(0)
    j = pl.program_id(1)
    tm, tn = o_ref.shape
    pad = SUBLANES
    n_q = GDN_QK_W // tn
    n_qk = 2 * GDN_QK_W // tn
    n_conv = (2 * GDN_QK_W + GDN_V_W) // tn

    @pl.when(j == 0)
    def _():
        x = x_ref[...]
        y = x * lax.rsqrt(jnp.mean(x * x, axis=-1, keepdims=True) + NORM_EPS)
        hb = (y * g_ref[...]).astype(BF16)
        h_ref[...] = hb
        g = _dot_nt(wgt_ref[...], hb)
        row = lax.broadcasted_iota(jnp.int32, g.shape, 0) % SUBLANES
        log_alpha = -jnp.exp(alog_ref[...]) * _softplus(g + dtb_ref[...])
        grow_ref[...] = jnp.where(row < GDN_REP, _sigmoid(g), _cumsum_lanes(log_alpha, GDN_CHUNK))

    pieces = [slice(c, c + PROJ_PIECE) for c in range(0, tn, PROJ_PIECE)]

    def conv_rows(t, cs):
        t1 = pltpu.roll(t, 1, 0)
        w = [cw_ref[s:s + 1, cs] for s in range(CONV_WIDTH)]
        assert CONV_WIDTH == 4
        return (w[3] * t + w[2] * t1) + pltpu.roll(w[1] * t + w[0] * t1, 2, 0)

    def stage_products():
        for cs, st in zip(pieces, stage_refs):
            st[...] = _dot(h_ref[...], w_ref[:, cs])

    def conv_silu(cs, st):
        r = st[...]
        prev = jnp.where(i % seq_tiles == 0, 0.0, carry_ref[j, :, cs])
        carry_ref[j, :, cs] = r[tm - pad:tm, :]
        top = conv_rows(jnp.concatenate([prev, r[0:pad, :]], axis=0), cs)[pad:2 * pad, :]
        return _silu(jnp.concatenate([top, conv_rows(r, cs)[pad:, :]], axis=0))

    @pl.when(j < n_qk)
    def _():
        sc = jnp.where(j < n_q, GDN_DK ** -0.5, 1.0)
        stage_products()
        for cs, st in zip(pieces, stage_refs):
            y = conv_silu(cs, st)
            for hh in range(PROJ_PIECE // GDN_DK):
                yh = y[:, hh * GDN_DK:(hh + 1) * GDN_DK]
                inv = lax.rsqrt(jnp.sum(yh * yh, axis=-1, keepdims=True) + NORM_EPS) * sc
                o_ref[:, cs.start + hh * GDN_DK:cs.start + (hh + 1) * GDN_DK] = (yh * inv).astype(o_ref.dtype)

    @pl.when((j >= n_qk) & (j < n_conv))
    def _():
        stage_products()
        for cs, st in zip(pieces, stage_refs):
            o_ref[:, cs] = conv_silu(cs, st).astype(o_ref.dtype)

    @pl.when(j >= n_conv)
    def _():
        for cs in pieces:
            o_ref[:, cs] = _silu(_dot(h_ref[...], w_ref[:, cs])).astype(o_ref.dtype)


def _norm_proj_gdn(x2d, gain, w, wgt, conv_w, a_log, dt_bias, *, tm, tn, seq):
    m, d = x2d.shape
    n = 2 * GDN_QK_W + 2 * GDN_V_W
    conv_c = conv_w.shape[1]
    assert m % tm == 0 and n % tn == 0 and seq % tm == 0 and tm % GDN_CHUNK == 0
    assert GDN_QK_W % tn == 0 and conv_c % tn == 0 and tn % GDN_DK == 0
    n_conv = conv_c // tn

    def per_head_rows(p):
        t = jnp.zeros((GDN_QK_HEADS, SUBLANES), F32)
        t = t.at[:, GDN_REP:2 * GDN_REP].set(p.astype(F32).reshape(GDN_QK_HEADS, GDN_REP))
        return t.reshape(GATE_ROWS, 1)

    return pl.pallas_call(
        functools.partial(_norm_proj_gdn_kernel, seq_tiles=seq // tm),
        grid=(m // tm, n // tn),
        in_specs=[
            pl.BlockSpec((tm, d), lambda i, j: (i, 0)),
            pl.BlockSpec((1, d), lambda i, j: (0, 0)),
            pl.BlockSpec((d, tn), lambda i, j: (0, j)),
            pl.BlockSpec((GATE_ROWS, d), lambda i, j: (0, 0)),
            pl.BlockSpec((CONV_WIDTH, tn), lambda i, j: (0, jnp.minimum(j, n_conv - 1))),
            pl.BlockSpec((GATE_ROWS, 1), lambda i, j: (0, 0)),
            pl.BlockSpec((GATE_ROWS, 1), lambda i, j: (0, 0)),
        ],
        out_specs=[
            pl.BlockSpec((tm, tn), lambda i, j: (i, j)),
            pl.BlockSpec((GATE_ROWS, tm), lambda i, j: (0, i)),
        ],
        out_shape=[
            jax.ShapeDtypeStruct((m, n), BF16),
            jax.ShapeDtypeStruct((GATE_ROWS, m), F32),
        ],
        scratch_shapes=[
            pltpu.VMEM((tm, d), BF16),
            pltpu.VMEM((n_conv, SUBLANES, tn), F32),
        ] + [pltpu.VMEM((tm, PROJ_PIECE), F32)] * (tn // PROJ_PIECE),
        compiler_params=pltpu.CompilerParams(
            dimension_semantics=("arbitrary", "arbitrary"), vmem_limit_bytes=VMEM_LIMIT),
        name="norm_proj_gdn",
    )(x2d, gain.reshape(1, d), w, wgt, conv_w, per_head_rows(a_log), per_head_rows(dt_bias))


def _out_proj_kernel(x_ref, *refs):
    o_ref = refs[-1]
    n_in = (len(refs) - 1) // 2
    acc = x_ref[...]
    for i in range(n_in):
        acc = acc + _dot(refs[i][...], refs[n_in + i][...])
    o_ref[...] = acc


def _out_proj(x2d, acts, w, *, tm):
    m, d = x2d.shape
    assert m % tm == 0 and sum(a.shape[1] for a in acts) == w.shape[0]
    offs = np.cumsum([0] + [a.shape[1] for a in acts])
    w_specs = []
    for a, off in zip(acts, offs):
        ka = a.shape[1]
        assert off % ka == 0
        w_specs.append(pl.BlockSpec((ka, d), functools.partial(lambda i, blk: (blk, 0), blk=int(off // ka)),
                                    pipeline_mode=pl.Buffered(1)))
    return pl.pallas_call(
        _out_proj_kernel,
        grid=(m // tm,),
        in_specs=(
            [pl.BlockSpec((tm, d), lambda i: (i, 0))]
            + [pl.BlockSpec((tm, a.shape[1]), lambda i: (i, 0)) for a in acts]
            + w_specs
        ),
        out_specs=pl.BlockSpec((tm, d), lambda i: (i, 0)),
        out_shape=jax.ShapeDtypeStruct((m, d), F32),
        compiler_params=pltpu.CompilerParams(
            dimension_semantics=("arbitrary",), vmem_limit_bytes=VMEM_LIMIT),
        name="out_proj",
    )(x2d, *acts, *([w] * len(acts)))


def _ffn_kernel(x_ref, g_ref, wg_ref, wu_ref, wd_ref, fg_ref, o_ref, h_ref, *, final_norm):
    j = pl.program_id(1)

    @pl.when(j == 0)
    def _():
        x = x_ref[...]
        y = x * lax.rsqrt(jnp.mean(x * x, axis=-1, keepdims=True) + NORM_EPS)
        h_ref[...] = (y * g_ref[...]).astype(BF16)
        o_ref[...] = x

    hb = h_ref[...]
    a = _dot(hb, wg_ref[...])
    b = _dot(hb, wu_ref[...])
    o_ref[...] += _dot((_silu(a) * b).astype(BF16), wd_ref[...])

    if final_norm:
        @pl.when(j == pl.num_programs(1) - 1)
        def _():
            x = o_ref[...]
            y = x * lax.rsqrt(jnp.mean(x * x, axis=-1, keepdims=True) + NORM_EPS)
            o_ref[...] = y * fg_ref[...]


def _ffn(x2d, gain, wg, wu, wd, final_gain, *, tm, tf, final_norm):
    m, d = x2d.shape
    f = wg.shape[1]
    assert m % tm == 0 and f % tf == 0
    return pl.pallas_call(
        functools.partial(_ffn_kernel, final_norm=final_norm),
        grid=(m // tm, f // tf),
        in_specs=[
            pl.BlockSpec((tm, d), lambda i, j: (i, 0)),
            pl.BlockSpec((1, d), lambda i, j: (0, 0)),
            pl.BlockSpec((d, tf), lambda i, j: (0, j)),
            pl.BlockSpec((d, tf), lambda i, j: (0, j)),
            pl.BlockSpec((tf, d), lambda i, j: (j, 0)),
            pl.BlockSpec((1, d), lambda i, j: (0, 0)),
        ],
        out_specs=pl.BlockSpec((tm, d), lambda i, j: (i, 0)),
        out_shape=jax.ShapeDtypeStruct((m, d), F32),
        scratch_shapes=[pltpu.VMEM((tm, d), BF16)],
        compiler_params=pltpu.CompilerParams(
            dimension_semantics=("arbitrary", "arbitrary"), vmem_limit_bytes=VMEM_LIMIT),
        name="ffn_final" if final_norm else "ffn",
    )(x2d, gain.reshape(1, d), wg, wu, wd, final_gain.reshape(1, d))


def _mlstm_kernel(q_ref, k_ref, v_ref, og_ref, grow_ref, norm_ref, o_ref, c_ref, n_ref, m_ref):
    L = q_ref.shape[0]
    nh = MLSTM_HEADS

    @pl.when(pl.program_id(1) == 0)
    def _():
        c_ref[...] = jnp.zeros_like(c_ref)
        n_ref[...] = jnp.zeros_like(n_ref)
        m_ref[...] = jnp.full_like(m_ref, -jnp.inf)

    rows = grow_ref[...]
    cols = _rows_to_cols(rows)

    li = lax.broadcasted_iota(jnp.int32, (L, L), 0)
    si = lax.broadcasted_iota(jnp.int32, (L, L), 1)
    causal = si <= li
    scale = MLSTM_DQK ** -0.5

    st = []
    for h in range(nh):
        li_r = rows[h:h + 1, :]
        bc_r = rows[nh + h:nh + h + 1, :]
        li_c = cols[:, h:h + 1]
        bc_c = cols[:, nh + h:nh + h + 1]
        g_tot = bc_c[L - 1:L, :]
        m_prev = m_ref[h][:, 0:1]
        log_d = jnp.where(causal, bc_c + (li_r - bc_r), -jnp.inf)
        m_inter = bc_c + m_prev
        m_t = jnp.maximum(m_inter, jnp.max(log_d, axis=-1, keepdims=True))
        q = q_ref[:, h * MLSTM_DQK:(h + 1) * MLSTM_DQK]
        k = k_ref[:, h * MLSTM_DQK:(h + 1) * MLSTM_DQK]
        st.append(dict(
            q=q, k=k, v=v_ref[:, h * MLSTM_DV:(h + 1) * MLSTM_DV], c_mat=c_ref[h], n_vec=n_ref[h],
            m_t=m_t, inter=jnp.exp(m_inter - m_t) * scale,
            s=_dot_nt(q, k) * scale * jnp.exp(log_d - m_t),
            log_w=g_tot + (li_c - bc_c), g_tot=g_tot, m_prev=m_prev))

    for d in st:
        num = _dot(d["s"].astype(BF16), d["v"]) + d["inter"] * _dot(d["q"], d["c_mat"].astype(BF16))
        qn = jnp.sum(d["q"].astype(F32) * d["n_vec"], axis=-1, keepdims=True)
        den = jnp.sum(d["s"], axis=-1, keepdims=True) + d["inter"] * qn
        d["hout"] = num * (1.0 / jnp.maximum(jnp.abs(den), jnp.exp(-d["m_t"])))

    for h, d in enumerate(st):
        y = _head_rms(d["hout"]) * norm_ref[:, h * MLSTM_DV:(h + 1) * MLSTM_DV]
        og = og_ref[:, h * MLSTM_DV:(h + 1) * MLSTM_DV].astype(F32)
        o_ref[:, h * MLSTM_DV:(h + 1) * MLSTM_DV] = (_sigmoid(og) * y).astype(o_ref.dtype)

    for h, d in enumerate(st):
        m_w = jnp.max(d["log_w"], axis=0, keepdims=True)
        m_new = jnp.maximum(d["g_tot"] + d["m_prev"], m_w)
        decay = jnp.exp(d["g_tot"] + d["m_prev"] - m_new)
        kw = d["k"].astype(F32) * jnp.exp(d["log_w"] - m_new)
        c_ref[h] = decay * d["c_mat"] + _dot_tn(kw.astype(BF16), d["v"])
        n_ref[h] = decay * d["n_vec"] + jnp.sum(kw, axis=0, keepdims=True)
        m_ref[h] = jnp.broadcast_to(m_new, (1, LANES))


def _mlstm(proj, grow, norm_w, *, batch, seq):
    m = proj.shape[0]
    L = L0_CHUNK
    nc = seq // L
    tok = lambda b, c: b * nc + c
    return pl.pallas_call(
        _mlstm_kernel,
        grid=(batch, nc),
        in_specs=[
            pl.BlockSpec((L, MLSTM_QK_W), lambda b, c: (tok(b, c), 0)),
            pl.BlockSpec((L, MLSTM_QK_W), lambda b, c: (tok(b, c), 1)),
            pl.BlockSpec((L, MLSTM_V_W), lambda b, c: (tok(b, c), 1)),
            pl.BlockSpec((L, MLSTM_V_W), lambda b, c: (tok(b, c), 2)),
            pl.BlockSpec((SUBLANES, L), lambda b, c: (0, tok(b, c))),
            pl.BlockSpec((1, MLSTM_V_W), lambda b, c: (0, 0)),
        ],
        out_specs=pl.BlockSpec((L, MLSTM_V_W), lambda b, c: (tok(b, c), 0)),
        out_shape=jax.ShapeDtypeStruct((m, MLSTM_V_W), BF16),
        scratch_shapes=[
            pltpu.VMEM((MLSTM_HEADS, MLSTM_DQK, MLSTM_DV), F32),
            pltpu.VMEM((MLSTM_HEADS, 1, MLSTM_DQK), F32),
            pltpu.VMEM((MLSTM_HEADS, 1, LANES), F32),
        ],
        compiler_params=pltpu.CompilerParams(
            dimension_semantics=("arbitrary", "arbitrary"), vmem_limit_bytes=VMEM_LIMIT),
        name="mlstm",
    )(proj, proj, proj, proj, grow, norm_w.reshape(1, MLSTM_V_W))


def _ret_log_gamma(h):
    return math.log(1.0 - 2.0 ** (-5.0 - h))


def _retention_kernel(q_ref, k_ref, v_ref, g_ref, pos_ref, invf_ref, o_ref, r_ref, d_ref):
    L = q_ref.shape[0]
    half = RET_DK // 2
    scale = RET_DK ** -0.5

    @pl.when((pl.program_id(0) == 0) & (pl.program_id(1) == 0))
    def _():
        li = lax.broadcasted_iota(jnp.int32, (L, L), 0)
        si = lax.broadcasted_iota(jnp.int32, (L, L), 1)
        rel = (li - si).astype(F32)
        for h in range(RET_HEADS):
            d_ref[h] = jnp.where(si <= li, jnp.exp(rel * _ret_log_gamma(h)), 0.0) * scale

    @pl.when(pl.program_id(1) == 0)
    def _():
        r_ref[...] = jnp.zeros_like(r_ref)

    ang_t = invf_ref[...] * pos_ref[...].astype(F32)
    cos = jnp.cos(ang_t).T
    sin = jnp.sin(ang_t).T
    p_c = lax.broadcasted_iota(jnp.int32, (L, 1), 0).astype(F32)

    def rot(t):
        t1 = t[:, :half].astype(F32)
        t2 = t[:, half:].astype(F32)
        return jnp.concatenate([t1 * cos - t2 * sin, t1 * sin + t2 * cos], axis=1)

    for h in range(RET_HEADS):
        lg = _ret_log_gamma(h)
        sl = slice(h * RET_DK, (h + 1) * RET_DK)
        qr = rot(q_ref[:, sl])
        kr = rot(k_ref[:, sl])
        v = v_ref[:, h * RET_DV:(h + 1) * RET_DV]
        qb = qr.astype(BF16)
        r_mat = r_ref[h]
        s = _dot_nt(qb, kr.astype(BF16)) * d_ref[h]
        intra = _dot(s.astype(BF16), v)
        inter = _dot(qb, r_mat.astype(BF16)) * (jnp.exp((p_c + 1.0) * lg) * scale)
        kw = kr * jnp.exp((L - 1.0 - p_c) * lg)
        r_ref[h] = math.exp(L * lg) * r_mat + _dot_tn(kw.astype(BF16), v)
        y = _head_rms(intra + inter) * _silu(g_ref[:, h * RET_DV:(h + 1) * RET_DV].astype(F32))
        o_ref[:, h * RET_DV:(h + 1) * RET_DV] = y.astype(o_ref.dtype)


def _retention(proj, positions, *, batch, seq):
    m = proj.shape[0]
    L = RET_CHUNK
    nc = seq // L
    tok = lambda b, c: b * nc + c
    half = RET_DK // 2
    inv_freq = ROPE_BASE ** (-jnp.arange(0, RET_DK, 2, dtype=F32) / RET_DK)
    base = (2 * MLSTM_QK_W + 2 * MLSTM_V_W) // RET_QK_W
    return pl.pallas_call(
        _retention_kernel,
        grid=(batch, nc),
        in_specs=[
            pl.BlockSpec((L, RET_QK_W), lambda b, c: (tok(b, c), base)),
            pl.BlockSpec((L, RET_QK_W), lambda b, c: (tok(b, c), base + 1)),
            pl.BlockSpec((L, RET_V_W), lambda b, c: (tok(b, c), base + 2)),
            pl.BlockSpec((L, RET_V_W), lambda b, c: (tok(b, c), base + 3)),
            pl.BlockSpec((None, 1, L), lambda b, c: (tok(b, c), 0, 0)),
            pl.BlockSpec((half, 1), lambda b, c: (0, 0)),
        ],
        out_specs=pl.BlockSpec((L, RET_V_W), lambda b, c: (tok(b, c), 0)),
        out_shape=jax.ShapeDtypeStruct((m, RET_V_W), BF16),
        scratch_shapes=[
            pltpu.VMEM((RET_HEADS, RET_DK, RET_DV), F32),
            pltpu.VMEM((RET_HEADS, L, L), F32),
        ],
        compiler_params=pltpu.CompilerParams(
            dimension_semantics=("arbitrary", "arbitrary"), vmem_limit_bytes=VMEM_LIMIT),
        name="retention",
    )(proj, proj, proj, proj, positions.reshape(batch * nc, 1, L), inv_freq.reshape(half, 1))


def _gdn_kernel(q_ref, k_ref, v_ref, z_ref, grow_ref, norm_ref, o_ref, s_ref):
    T = q_ref.shape[0]
    C = GDN_CHUNK
    R = GDN_REP * C
    nch = T // C
    assert GDN_REP == 2 and R == LANES

    @pl.when(pl.program_id(2) == 0)
    def _():
        s_ref[...] = jnp.zeros_like(s_ref)

    rows = grow_ref[...]
    cols = _rows_to_cols(rows)

    ri = lax.broadcasted_iota(jnp.int32, (R, R), 0)
    ci = lax.broadcasted_iota(jnp.int32, (R, R), 1)
    same = (ri // C) == (ci // C)
    causal = same & (ci <= ri)
    strict = same & (ci < ri)
    eye = jnp.where(ri == ci, 1.0, 0.0)
    top = lax.broadcasted_iota(jnp.int32, (R, GDN_DV), 0) < C

    def stack2(a, b):
        return jnp.concatenate([a, b], axis=0)

    pre = {}

    def prepare(c, h):
        rs = slice(c * C, (c + 1) * C)
        ks = slice(h * GDN_DK, (h + 1) * GDN_DK)
        q_c = q_ref[rs, ks]
        k_c = k_ref[rs, ks]
        k2b = stack2(k_c, k_c)
        q2b = stack2(q_c, q_c)
        k2 = k2b.astype(F32)
        q2 = q2b.astype(F32)
        base = h * SUBLANES
        beta2 = stack2(cols[rs, base:base + 1], cols[rs, base + 1:base + 2])
        gc2 = stack2(cols[rs, base + GDN_REP:base + GDN_REP + 1], cols[rs, base + GDN_REP + 1:base + GDN_REP + 2])
        gc2_r = jnp.concatenate([rows[base + GDN_REP:base + GDN_REP + 1, rs],
                                 rows[base + GDN_REP + 1:base + GDN_REP + 2, rs]], axis=1)
        gl0 = gc2[C - 1:C, :]
        gl1 = gc2[R - 1:R, :]
        bt_w = jnp.broadcast_to(beta2, (R, LANES))
        gc_w = jnp.broadcast_to(gc2, (R, LANES))
        gl_w = jnp.where(top, gl0, gl1)
        eg_w = jnp.exp(gc_w)
        decay = jnp.exp(jnp.where(causal, gc_w - gc2_r, -jnp.inf))
        kq = _dot_nt(jnp.concatenate([k2b, q2b], axis=0), k2b)
        kk = kq[:R]
        qk = kq[R:]
        v2 = stack2(v_ref[rs, (2 * h) * GDN_DV:(2 * h + 1) * GDN_DV],
                    v_ref[rs, (2 * h + 1) * GDN_DV:(2 * h + 2) * GDN_DV]).astype(F32)
        pre[(c, h)] = dict(
            a=jnp.where(strict, bt_w * kk * decay, 0.0),
            attn=(qk * decay).astype(BF16),
            rhs=jnp.concatenate([v2 * bt_w, k2 * (bt_w * eg_w)], axis=1).astype(BF16),
            qg=(q2 * eg_w).astype(BF16),
            kt=(k2 * jnp.exp(gl_w - gc_w)).astype(BF16),
            gt=jnp.concatenate([jnp.broadcast_to(jnp.exp(gl0), (1, GDN_DV)),
                                jnp.broadcast_to(jnp.exp(gl1), (1, GDN_DV))], axis=1),
        )

    def solve(pairs):
        p = {}
        t = {}
        for key in pairs:
            p0 = -pre[key]["a"]
            pb = p0.astype(BF16)
            t[key] = eye + p0
            p[key] = _dot(pb, pb)
        step = 4
        while step < C:
            for key in pairs:
                pb = p[key].astype(BF16)
                x = _dot(jnp.concatenate([t[key].astype(BF16), pb], axis=0), pb)
                t[key] = t[key] + x[:R]
                p[key] = x[R:]
            step *= 2
        for key in pairs:
            t[key] = t[key] + _dot(t[key].astype(BF16), p[key].astype(BF16))
        return {key: _dot(t[key].astype(BF16), pre[key]["rhs"]) for key in pairs}

    s2 = [s_ref[h] for h in range(GDN_GROUP)]

    def recur(c, uw):
        rs = slice(c * C, (c + 1) * C)
        x1 = []
        for h in range(GDN_GROUP):
            w2b = uw[(c, h)][:, GDN_DV:].astype(BF16)
            lhs = jnp.concatenate([w2b, pre[(c, h)]["qg"]], axis=0)
            x1.append(_dot(lhs, s2[h].astype(BF16)))
        for h in range(GDN_GROUP):
            d = pre[(c, h)]
            ws = jnp.where(top, x1[h][:R, :GDN_DV], x1[h][:R, GDN_DV:])
            qs = jnp.where(top, x1[h][R:, :GDN_DV], x1[h][R:, GDN_DV:])
            vn = uw[(c, h)][:, :GDN_DV] - ws
            out = qs + _dot(d["attn"], vn.astype(BF16))
            vx = jnp.concatenate([jnp.where(top, vn, 0.0), jnp.where(top, 0.0, vn)], axis=1).astype(BF16)
            s2[h] = s2[h] * d["gt"] + _dot_tn(d["kt"], vx)
            z2 = stack2(z_ref[rs, (2 * h) * GDN_DV:(2 * h + 1) * GDN_DV],
                        z_ref[rs, (2 * h + 1) * GDN_DV:(2 * h + 2) * GDN_DV]).astype(F32)
            y = (_head_rms(out) * norm_ref[...] * z2).astype(o_ref.dtype)
            o_ref[rs, (2 * h) * GDN_DV:(2 * h + 1) * GDN_DV] = y[:C]
            o_ref[rs, (2 * h + 1) * GDN_DV:(2 * h + 2) * GDN_DV] = y[C:]

    groups = [[(c, h) for c in range(c0, c0 + GDN_PHASE_CHUNKS) for h in range(GDN_GROUP)]
              for c0 in range(0, nch, GDN_PHASE_CHUNKS)]
    for key in groups[0]:
        prepare(*key)
    uw = solve(groups[0])
    for gi, group in enumerate(groups):
        nxt = groups[gi + 1] if gi + 1 < len(groups) else []
        chunks = sorted({c for c, _ in group})
        recur(chunks[0], uw)
        for key in nxt:
            prepare(*key)
        for c in chunks[1:]:
            recur(c, uw)
        if nxt:
            uw = solve(nxt)
    for h in range(GDN_GROUP):
        s_ref[h] = s2[h]


def _gdn(proj, grow, norm_w, *, batch, seq):
    m = proj.shape[0]
    T = GDN_BLOCK
    G = GDN_GROUP
    nt = seq // T
    ng = GDN_QK_HEADS // G
    qw = G * GDN_DK
    vw = G * GDN_REP * GDN_DV
    tok = lambda b, h, t: b * nt + t
    kbase = GDN_QK_W // qw
    vbase = 2 * GDN_QK_W // vw
    zbase = (2 * GDN_QK_W + GDN_V_W) // vw
    return pl.pallas_call(
        _gdn_kernel,
        grid=(batch, ng, nt),
        in_specs=[
            pl.BlockSpec((T, qw), lambda b, h, t: (tok(b, h, t), h)),
            pl.BlockSpec((T, qw), lambda b, h, t: (tok(b, h, t), kbase + h)),
            pl.BlockSpec((T, vw), lambda b, h, t: (tok(b, h, t), vbase + h)),
            pl.BlockSpec((T, vw), lambda b, h, t: (tok(b, h, t), zbase + h)),
            pl.BlockSpec((G * SUBLANES, T), lambda b, h, t: (h, tok(b, h, t))),
            pl.BlockSpec((1, GDN_DV), lambda b, h, t: (0, 0)),
        ],
        out_specs=pl.BlockSpec((T, vw), lambda b, h, t: (tok(b, h, t), h)),
        out_shape=jax.ShapeDtypeStruct((m, GDN_V_W), BF16),
        scratch_shapes=[pltpu.VMEM((G, GDN_DK, GDN_REP * GDN_DV), F32)],
        compiler_params=pltpu.CompilerParams(
            dimension_semantics=("arbitrary", "arbitrary", "arbitrary"), vmem_limit_bytes=VMEM_LIMIT),
        name="gdn",
    )(proj, proj, proj, proj, grow, norm_w.reshape(1, GDN_DV))


def _l0_weights(w_in):
    d = w_in.shape[0]
    o = np.cumsum([0, MLSTM_QK_W, MLSTM_QK_W, MLSTM_V_W, MLSTM_V_W, 2 * MLSTM_HEADS,
                   RET_QK_W, RET_QK_W, RET_V_W, RET_V_W])
    src = np.concatenate([h * RET_DK + np.concatenate([np.arange(0, RET_DK, 2), np.arange(1, RET_DK, 2)])
                          for h in range(RET_HEADS)])
    perm = jnp.asarray(np.arange(RET_QK_W)[:, None] == src[None, :], dtype=BF16)

    def deinterleave(w):
        return jnp.dot(w, perm, preferred_element_type=BF16)

    wb = w_in.astype(BF16)
    w_main = jnp.concatenate([
        wb[:, o[0]:o[4]], deinterleave(wb[:, o[5]:o[6]]), deinterleave(wb[:, o[6]:o[7]]), wb[:, o[7]:o[9]],
    ], axis=1)
    return w_main, wb[:, o[4]:o[5]].T


def _l1_weights(w_in):
    d = w_in.shape[0]
    main_w = 2 * GDN_QK_W + 2 * GDN_V_W
    w_main = w_in.astype(BF16)
    b_pre =w_in[:, main_w:main_w + GDN_V_HEADS].T.reshape(GDN_QK_HEADS, GDN_REP, d)
    a_pre = w_in[:, main_w + GDN_V_HEADS:main_w + 2 * GDN_V_HEADS].T.reshape(GDN_QK_HEADS, GDN_REP, d)
    wgt = jnp.zeros((GDN_QK_HEADS, SUBLANES, d), F32)
    wgt = wgt.at[:, 0:GDN_REP].set(b_pre).at[:, GDN_REP:2 * GDN_REP].set(a_pre)
    return w_main, wgt.reshape(GATE_ROWS, d).astype(BF16)


def kernel(x, positions, l0_norm_mix, l0_w_in, l0_gate_bias, l0_mlstm_norm, l0_w_out, l0_norm_ffn, l0_w_gate,
           l0_w_up, l0_w_down, l1_norm_mix, l1_w_in, l1_conv_w, l1_a_log, l1_dt_bias, l1_gdn_norm, l1_w_out,
           l1_norm_ffn, l1_w_gate, l1_w_up, l1_w_down, final_norm):
    batch, seq, d = x.shape
    m = batch * seq
    x0 = x.reshape(m, d)

    w0, wgt0 = _l0_weights(l0_w_in)
    proj0, grow0 = _norm_proj(x0, l0_norm_mix, w0, wgt0, l0_gate_bias, tm=1024, tn=1024)
    h_m = _mlstm(proj0, grow0, l0_mlstm_norm, batch=batch, seq=seq)
    y_r = _retention(proj0, positions, batch=batch, seq=seq)
    x1 = _out_proj(x0, [h_m, y_r], l0_w_out.astype(BF16), tm=512)
    x1 = _ffn(x1, l0_norm_ffn, l0_w_gate.astype(BF16), l0_w_up.astype(BF16), l0_w_down.astype(BF16),
              final_norm, tm=512, tf=512, final_norm=False)

    w1, wgt1 = _l1_weights(l1_w_in)
    proj1, grow1 = _norm_proj_gdn(x1, l1_norm_mix, w1, wgt1, l1_conv_w, l1_a_log, l1_dt_bias,
                                  tm=1024, tn=1024, seq=seq)
    o = _gdn(proj1, grow1, l1_gdn_norm, batch=batch, seq=seq)
    x2 = _out_proj(x1, [o], l1_w_out.astype(BF16), tm=512)
    out = _ffn(x2, l1_norm_ffn, l1_w_gate.astype(BF16), l1_w_up.astype(BF16), l1_w_down.astype(BF16),
               final_norm, tm=512, tf=512, final_norm=True)
    return out.reshape(batch, seq, d)
```
